```python
import math
import jax, jax.numpy as jnp
from jax import lax
import numpy as np

D_MODEL = 1024
BATCH = 8
SEQ = 2048
DEPTH = 4
DEC_BATCH = 128
DEC_SEQ = 8
PAST_LEN = 2048
PAGE_SIZE = 128

HEAD_DIM = 64
NSA_HEADS = 8
NSA_KV_HEADS = 2
NSA_GROUP = NSA_HEADS // NSA_KV_HEADS
NSA_WIDTH = NSA_HEADS * HEAD_DIM
KV_WIDTH = NSA_KV_HEADS * HEAD_DIM
NSA_COLS = NSA_WIDTH + 6 * KV_WIDTH + 3 * NSA_HEADS
RWKV_HEADS = 8
RWKV_WIDTH = RWKV_HEADS * HEAD_DIM
MIX_WIDTH = NSA_WIDTH + RWKV_WIDTH
DECAY_LORA = 64
ICLR_LORA = 64
GATE_LORA = 128
RW_COLS = 3 * RWKV_WIDTH + DECAY_LORA + ICLR_LORA + GATE_LORA
RW_SPLITS = (RWKV_WIDTH, RWKV_WIDTH + DECAY_LORA, 2 * RWKV_WIDTH + DECAY_LORA,
             3 * RWKV_WIDTH + DECAY_LORA, 3 * RWKV_WIDTH + DECAY_LORA + ICLR_LORA)
IN_COLS = NSA_COLS + RW_COLS
CMP_BLOCK = 32
CMP_STRIDE = 16
CMP_HIDDEN = 256
SEL_BLOCK = 64
SEL_TOPK = 8
N_LOCAL_BLOCKS = 2
WINDOW = 256
QUERY_BLOCK = 128
D_FF = 2752
CONV_WIDTH = 3
ROPE_THETA = 10000.0
NORM_EPS = 1e-6
GN_EPS = 64e-5
NEG_BIG = -1e30

kernel_name = 'nsa_rwkv7_parallel_heads_convffn_step'


def rms_norm(x, g):
    xf = x.astype(jnp.float32)
    y = xf * lax.rsqrt(jnp.mean(xf * xf, axis=-1, keepdims=True) + NORM_EPS)
    return (y * g.astype(jnp.float32)).astype(x.dtype)


def rope(x, pos):
    half = HEAD_DIM // 2
    inv = ROPE_THETA ** (-jnp.arange(half, dtype=jnp.float32) / half)
    ang = pos.astype(jnp.float32)[:, None] * inv[None, :]
    cos = jnp.cos(ang)[None, :, None, :]
    sin = jnp.sin(ang)[None, :, None, :]
    xf = x.astype(jnp.float32)
    x1, x2 = xf[..., :half], xf[..., half:]
    return jnp.concatenate([x1 * cos - x2 * sin, x2 * cos + x1 * sin], axis=-1).astype(x.dtype)


def masked_softmax(s, mask):
    s = jnp.where(mask, s.astype(jnp.float32), NEG_BIG)
    m = jnp.max(s, axis=-1, keepdims=True)
    e = jnp.where(mask, jnp.exp(s - m), 0.0)
    return e / jnp.maximum(jnp.sum(e, axis=-1, keepdims=True), 1e-30)


def compress_blocks(rows, pe, w1, w2):
    b, length = rows.shape[:2]
    n_cmp = (length - CMP_BLOCK) // CMP_STRIDE + 1
    idx = jnp.arange(n_cmp)[:, None] * CMP_STRIDE + jnp.arange(CMP_BLOCK)[None, :]
    blocks = jnp.take(rows, idx, axis=1) + pe[:, None, :]
    flat = blocks.transpose(0, 1, 3, 2, 4).reshape(b, n_cmp, NSA_KV_HEADS, CMP_BLOCK * HEAD_DIM)
    return jax.nn.silu(flat @ w1) @ w2


def nsa_attention(q, gates, kv_all, win_ext, pe, w1, w2):
    b, t = q.shape[:2]
    dt = q.dtype
    length = kv_all.shape[1]
    past_len = length - t
    scale = HEAD_DIM ** -0.5
    k_cmp = compress_blocks(kv_all[:, :, 0], pe[0], w1[0], w2[0])
    v_cmp = compress_blocks(kv_all[:, :, 1], pe[1], w1[1], w2[1])
    n_cmp = k_cmp.shape[1]
    cmp_start = jnp.arange(n_cmp) * CMP_STRIDE
    cmp_end = cmp_start + CMP_BLOCK - 1
    n_sel = -(-length // SEL_BLOCK)
    sel_start = jnp.arange(n_sel) * SEL_BLOCK
    blk = jnp.arange(n_sel)
    overlap = ((cmp_start[:, None] < sel_start[None, :] + SEL_BLOCK)
               & (cmp_end[:, None] >= sel_start[None, :])).astype(jnp.float32)
    kv_sel = jnp.pad(kv_all[:, :, 2:4], ((0, 0), (0, n_sel * SEL_BLOCK - length), (0, 0), (0, 0), (0, 0)))
    sel_blocks = kv_sel.reshape(b, n_sel, SEL_BLOCK, 2, NSA_KV_HEADS, HEAD_DIM)
    sel_blocks = sel_blocks.transpose(0, 4, 1, 2, 3, 5).reshape(b, NSA_KV_HEADS, n_sel, SEL_BLOCK * 2 * HEAD_DIM)
    topk = min(SEL_TOPK, n_sel)
    qb = math.gcd(t, QUERY_BLOCK)
    nb = t // qb

    def block(args):
        q_blk, g_blk, bi = args
        s0 = bi * qb
        qpos = past_len + s0 + jnp.arange(qb)
        qg = q_blk.reshape(b, qb, NSA_KV_HEADS, NSA_GROUP, HEAD_DIM)
        s_c = jnp.einsum('bqgrd,bcgd->bqgrc', qg, k_cmp) * scale
        p_c = masked_softmax(s_c, (cmp_end[None, :] <= qpos[:, None])[None, :, None, None, :])
        o_c = jnp.einsum('bqgrc,bcgd->bqgrd', p_c.astype(dt), v_cmp)
        imp = jnp.einsum('bqgrc,cs->bqgs', p_c, overlap)
        cur = qpos // SEL_BLOCK
        forced = (blk[None, :] == 0) | ((blk[None, :] <= cur[:, None]) & (blk[None, :] > cur[:, None] - N_LOCAL_BLOCKS))
        future = sel_start[None, :] > qpos[:, None]
        score = jnp.where(forced[None, :, None, :], 1e9, jnp.where(future[None, :, None, :], -1e9, imp))
        _, idx = lax.top_k(score, topk)
        idx_t = idx.transpose(0, 2, 1, 3).reshape(b, NSA_KV_HEADS, qb * topk)
        gath = jnp.take_along_axis(sel_blocks, idx_t[..., None], axis=2)
        gath = gath.reshape(b, NSA_KV_HEADS, qb, topk * SEL_BLOCK, 2, HEAD_DIM)
        kpos = (idx[..., None] * SEL_BLOCK + jnp.arange(SEL_BLOCK)).reshape(b, qb, NSA_KV_HEADS, topk * SEL_BLOCK)
        s_s = jnp.einsum('bqgrd,bgqkd->bqgrk', qg, gath[..., 0, :]) * scale
        p_s = masked_softmax(s_s, (kpos <= qpos[None, :, None, None])[:, :, :, None, :])
        o_s = jnp.einsum('bqgrk,bgqkd->bqgrd', p_s.astype(dt), gath[..., 1, :])
        win = lax.dynamic_slice_in_dim(win_ext, s0, WINDOW + qb, axis=1)
        wpos = past_len - WINDOW + s0 + jnp.arange(WINDOW + qb)
        wmask = (wpos[None, :] >= 0) & (wpos[None, :] <= qpos[:, None]) & (wpos[None, :] > qpos[:, None] - WINDOW)
        s_w = jnp.einsum('bqgrd,bkgd->bqgrk', qg, win[:, :, 0]) * scale
        p_w = masked_softmax(s_w, wmask[None, :, None, None, :])
        o_w = jnp.einsum('bqgrk,bkgd->bqgrd', p_w.astype(dt), win[:, :, 1])
        g = jax.nn.sigmoid(g_blk.astype(jnp.float32)).reshape(b, qb, NSA_KV_HEADS, NSA_GROUP, 3)
        o = g[..., 0:1] * o_c + g[..., 1:2] * o_s + g[..., 2:3] * o_w
        return o.reshape(b, qb, NSA_WIDTH).astype(dt)

    q_blocks = q.reshape(b, nb, qb, NSA_HEADS, HEAD_DIM).swapaxes(0, 1)
    g_blocks = gates.reshape(b, nb, qb, NSA_HEADS, 3).swapaxes(0, 1)
    out = lax.map(block, (q_blocks, g_blocks, jnp.arange(nb)))
    return out.swapaxes(0, 1).reshape(b, t, NSA_WIDTH)


def rwkv_mixer(p_rw, shift_prev, wkv_prev, p):
    b, t = p_rw.shape[:2]
    f32 = jnp.float32
    dt = p_rw.dtype
    prev = jnp.concatenate([shift_prev[:, None].astype(dt), p_rw[:, :-1]], axis=1)
    xm = p_rw + (prev - p_rw) * p['rw_mu']
    r, wa, k, v, aa, ga = jnp.split(xm, list(RW_SPLITS), axis=-1)
    w = -jax.nn.softplus(-(p['rw_w0'] + jnp.tanh(wa) @ p['rw_w_b']).astype(f32)) - 0.5
    decay = jnp.exp(-jnp.exp(w))
    a = jax.nn.sigmoid((p['rw_a0'] + aa @ p['rw_a_b']).astype(f32))
    g = (jax.nn.sigmoid(ga) @ p['rw_g_b']).astype(f32)

    def heads(z):
        return z.astype(f32).reshape(b, t, RWKV_HEADS, HEAD_DIM)

    r_h, v_h, w_h, a_h = heads(r), heads(v), heads(decay), heads(a)
    kk = heads(k * p['rw_k_k'])
    kk = kk / jnp.maximum(jnp.sqrt(jnp.sum(kk * kk, axis=-1, keepdims=True)), 1e-12)
    k_a = p['rw_k_a'].astype(f32).reshape(RWKV_HEADS, HEAD_DIM)
    k_h = heads(k) * (1.0 + (a_h - 1.0) * k_a)

    def step(s, inp):
        rt, wt, kt, vt, at, bt = inp
        sa = jnp.einsum('bhij,bhj->bhi', s, at)
        s = s * wt[:, :, None, :] + sa[..., None] * bt[:, :, None, :] + vt[..., None] * kt[:, :, None, :]
        return s, jnp.einsum('bhij,bhj->bhi', s, rt)

    xs = tuple(z.swapaxes(0, 1) for z in (r_h, w_h, k_h, v_h, -kk, kk * a_h))
    s_fin, ys = lax.scan(step, wkv_prev.astype(f32), xs)
    y = ys.swapaxes(0, 1)
    mu = jnp.mean(y, axis=-1, keepdims=True)
    var = jnp.mean(jnp.square(y - mu), axis=-1, keepdims=True)
    yn = ((y - mu) * lax.rsqrt(var + GN_EPS)).reshape(b, t, RWKV_WIDTH)
    yn = yn * p['rw_gn_w'].astype(f32) + p['rw_gn_b'].astype(f32)
    bonus = jnp.sum(r_h * k_h * p['rw_r_k'].astype(f32), axis=-1, keepdims=True) * v_h
    out = (yn + bonus.reshape(b, t, RWKV_WIDTH)) * g
    return out.astype(dt), s_fin, p_rw[:, -1]


def conv_ffn(xn, conv_prev, p):
    t = xn.shape[1]
    gate = xn @ p['ffn_w_gate']
    up = xn @ p['ffn_w_up']
    ext = jnp.concatenate([conv_prev.astype(gate.dtype), gate], axis=1)
    cw = p['ffn_conv_w']
    c = p['ffn_conv_b'] + sum(cw[j] * ext[:, j:j + t] for j in range(CONV_WIDTH))
    h = jax.nn.silu(c) * up
    return h @ p['ffn_w_down'], ext[:, t:]


def layer(x, past_kv, win_prev, wkv_prev, shift_prev, conv_prev, p):
    b, t = x.shape[:2]
    past_len = past_kv.shape[1]
    pos = past_len + jnp.arange(t)
    xn = rms_norm(x, p['norm_mix'])
    proj = xn @ p['w_in']
    q = rope(proj[..., :NSA_WIDTH].reshape(b, t, NSA_HEADS, HEAD_DIM), pos)
    kv = proj[..., NSA_WIDTH:NSA_WIDTH + 6 * KV_WIDTH].reshape(b, t, 3, 2, NSA_KV_HEADS, HEAD_DIM)
    gates = proj[..., NSA_WIDTH + 6 * KV_WIDTH:NSA_COLS].reshape(b, t, NSA_HEADS, 3)
    p_rw = proj[..., NSA_COLS:]
    keys = rope(kv[:, :, :, 0].reshape(b, t, 3 * NSA_KV_HEADS, HEAD_DIM), pos).reshape(b, t, 3, NSA_KV_HEADS, HEAD_DIM)
    kv = jnp.stack([keys, kv[:, :, :, 1]], axis=3).reshape(b, t, 6, NSA_KV_HEADS, HEAD_DIM)
    rows = kv[:, :, :4]
    win_rows = kv[:, :, 4:]
    kv_all = jnp.concatenate([past_kv.astype(rows.dtype), rows], axis=1)
    win_valid = jnp.concatenate([win_prev.astype(rows.dtype), win_rows], axis=1)
    wb = win_prev.shape[1]
    win_ext = jnp.pad(win_valid, ((0, 0), (WINDOW - wb, 0), (0, 0), (0, 0), (0, 0)))
    n_keep = min(WINDOW, win_valid.shape[1])
    new_win = win_valid[:, win_valid.shape[1] - n_keep:]
    o_nsa = nsa_attention(q, gates, kv_all, win_ext, p['cmp_pe'], p['cmp_w1'], p['cmp_w2'])
    o_rw, wkv_new, shift_new = rwkv_mixer(p_rw, shift_prev, wkv_prev, p)
    x = x + jnp.concatenate([o_nsa, o_rw], axis=-1) @ p['w_out']
    f, conv_new = conv_ffn(rms_norm(x, p['norm_ffn']), conv_prev, p)
    x = x + f
    return x, rows, new_win, wkv_new, shift_new, conv_new


def setup_inputs(seed: int = 0) -> dict:
    key = jax.random.key(seed)
    keys = iter(jax.random.split(key, 40))
    f32 = jnp.float32

    def nrm(shape, scale):
        return jax.random.normal(next(keys), shape, f32) * scale

    n_pages = PAST_LEN // PAGE_SIZE
    in_use = DEC_BATCH * n_pages
    n_phys = in_use + max(1, in_use // 4)
    perm = jax.random.permutation(next(keys), n_phys)
    page_table = perm[:in_use].reshape(DEC_BATCH, n_pages).astype(jnp.int32)
    win_buf = min(WINDOW, PAST_LEN)
    return {
        'x_prompt': nrm((BATCH, SEQ, D_MODEL), 1.0),
        'x_sample': nrm((DEC_BATCH, DEC_SEQ, D_MODEL), 1.0),
        'cache_kv': nrm((DEPTH, n_phys, PAGE_SIZE, 4, NSA_KV_HEADS, HEAD_DIM), 1.0),
        'page_table': page_table,
        'state_win': nrm((DEPTH, DEC_BATCH, win_buf, 2, NSA_KV_HEADS, HEAD_DIM), 1.0),
        'state_wkv': nrm((DEPTH, DEC_BATCH, RWKV_HEADS, HEAD_DIM, HEAD_DIM), 0.3),
        'state_shift': nrm((DEPTH, DEC_BATCH, RW_COLS), 1.0),
        'state_conv': nrm((DEPTH, DEC_BATCH, CONV_WIDTH - 1, D_FF), 1.0),
        'norm_mix': 1.0 + nrm((DEPTH, D_MODEL), 0.02),
        'w_in': nrm((DEPTH, D_MODEL, IN_COLS), D_MODEL ** -0.5),
        'cmp_pe': nrm((DEPTH, 2, CMP_BLOCK, HEAD_DIM), 0.1),
        'cmp_w1': nrm((DEPTH, 2, CMP_BLOCK * HEAD_DIM, CMP_HIDDEN), (CMP_BLOCK * HEAD_DIM) ** -0.5),
        'cmp_w2': nrm((DEPTH, 2, CMP_HIDDEN, HEAD_DIM), 2.0 * CMP_HIDDEN ** -0.5),
        'rw_mu': jax.random.uniform(next(keys), (DEPTH, RW_COLS), f32),
        'rw_w0': nrm((DEPTH, RWKV_WIDTH), 0.5),
        'rw_w_b': nrm((DEPTH, DECAY_LORA, RWKV_WIDTH), 0.5 * DECAY_LORA ** -0.5),
        'rw_a0': nrm((DEPTH, RWKV_WIDTH), 0.3),
        'rw_a_b': nrm((DEPTH, ICLR_LORA, RWKV_WIDTH), 0.5 * ICLR_LORA ** -0.5),
        'rw_g_b': nrm((DEPTH, GATE_LORA, RWKV_WIDTH), GATE_LORA ** -0.5),
        'rw_k_k': 0.85 + nrm((DEPTH, RWKV_WIDTH), 0.05),
        'rw_k_a': 1.0 + nrm((DEPTH, RWKV_WIDTH), 0.05),
        'rw_r_k': nrm((DEPTH, RWKV_HEADS, HEAD_DIM), 0.3),
        'rw_gn_w': 1.0 + nrm((DEPTH, RWKV_WIDTH), 0.02),
        'rw_gn_b': nrm((DEPTH, RWKV_WIDTH), 0.02),
        'w_out': nrm((DEPTH, MIX_WIDTH, D_MODEL), MIX_WIDTH ** -0.5),
        'norm_ffn': 1.0 + nrm((DEPTH, D_MODEL), 0.02),
        'ffn_w_gate': nrm((DEPTH, D_MODEL, D_FF), D_MODEL ** -0.5),
        'ffn_w_up': nrm((DEPTH, D_MODEL, D_FF), D_MODEL ** -0.5),
        'ffn_conv_w': nrm((DEPTH, CONV_WIDTH, D_FF), 0.6),
        'ffn_conv_b': nrm((DEPTH, D_FF), 0.02),
        'ffn_w_down': nrm((DEPTH, D_FF, D_MODEL), D_FF ** -0.5),
        'norm_final': 1.0 + nrm((D_MODEL,), 0.02),
    }


def reference(x_prompt, x_sample, cache_kv, page_table, state_win, state_wkv, state_shift, state_conv,
              norm_mix, w_in, cmp_pe, cmp_w1, cmp_w2, rw_mu, rw_w0, rw_w_b, rw_a0, rw_a_b, rw_g_b,
              rw_k_k, rw_k_a, rw_r_k, rw_gn_w, rw_gn_b, w_out, norm_ffn, ffn_w_gate, ffn_w_up,
              ffn_conv_w, ffn_conv_b, ffn_w_down, norm_final):
    dt = x_prompt.dtype
    bp = x_prompt.shape[0]
    bd = x_sample.shape[0]
    past_rows = page_table.shape[1] * PAGE_SIZE
    x_p, x_s = x_prompt, x_sample
    rows_p, rows_s, win_p, win_s, wkv_p, wkv_s, sh_p, sh_s, cv_p, cv_s = ([] for _ in range(10))
    for l in range(DEPTH):
        p = {
            'norm_mix': norm_mix[l], 'w_in': w_in[l], 'cmp_pe': cmp_pe[l], 'cmp_w1': cmp_w1[l],
            'cmp_w2': cmp_w2[l], 'rw_mu': rw_mu[l], 'rw_w0': rw_w0[l], 'rw_w_b': rw_w_b[l],
            'rw_a0': rw_a0[l], 'rw_a_b': rw_a_b[l], 'rw_g_b': rw_g_b[l], 'rw_k_k': rw_k_k[l],
            'rw_k_a': rw_k_a[l], 'rw_r_k': rw_r_k[l], 'rw_gn_w': rw_gn_w[l], 'rw_gn_b': rw_gn_b[l],
            'w_out': w_out[l], 'norm_ffn': norm_ffn[l], 'ffn_w_gate': ffn_w_gate[l],
            'ffn_w_up': ffn_w_up[l], 'ffn_conv_w': ffn_conv_w[l], 'ffn_conv_b': ffn_conv_b[l],
            'ffn_w_down': ffn_w_down[l],
        }
        x_p, r_, w_, s_, h_, c_ = layer(
            x_p,
            jnp.zeros((bp, 0, 4, NSA_KV_HEADS, HEAD_DIM), dt),
            jnp.zeros((bp, 0, 2, NSA_KV_HEADS, HEAD_DIM), dt),
            jnp.zeros((bp, RWKV_HEADS, HEAD_DIM, HEAD_DIM), jnp.float32),
            jnp.zeros((bp, RW_COLS), dt),
            jnp.zeros((bp, CONV_WIDTH - 1, D_FF), dt),
            p)
        rows_p.append(r_); win_p.append(w_); wkv_p.append(s_); sh_p.append(h_); cv_p.append(c_)
        past_kv = cache_kv[l][page_table].reshape(bd, past_rows, 4, NSA_KV_HEADS, HEAD_DIM)
        x_s, r_, w_, s_, h_, c_ = layer(x_s, past_kv, state_win[l], state_wkv[l], state_shift[l], state_conv[l], p)
        rows_s.append(r_); win_s.append(w_); wkv_s.append(s_); sh_s.append(h_); cv_s.append(c_)
    y_prompt = rms_norm(x_p, norm_final)
    y_sample = rms_norm(x_s, norm_final)
    return (y_prompt, y_sample,
            jnp.stack(rows_p), jnp.stack(rows_s),
            jnp.stack(win_p), jnp.stack(win_s),
            jnp.stack(wkv_p), jnp.stack(wkv_s),
            jnp.stack(sh_p), jnp.stack(sh_s),
            jnp.stack(cv_p), jnp.stack(cv_s))
```

```python
import functools
import math

import jax
import jax.numpy as jnp
from jax import lax
from jax.experimental import pallas as pl
from jax.experimental.pallas import tpu as pltpu

F32 = jnp.float32
BF16 = jnp.bfloat16

D_MODEL = 1024
PAGE_SIZE = 128
HEAD_DIM = 64
NSA_HEADS = 8
NSA_KV_HEADS = 2
NSA_GROUP = NSA_HEADS // NSA_KV_HEADS
NSA_WIDTH = NSA_HEADS * HEAD_DIM
KV_WIDTH = NSA_KV_HEADS * HEAD_DIM
N_GATES = 3 * NSA_HEADS
RWKV_HEADS = 8
RWKV_WIDTH = RWKV_HEADS * HEAD_DIM
DECAY_LORA = 64
ICLR_LORA = 64
GATE_LORA = 128
RW_COLS = 3 * RWKV_WIDTH + DECAY_LORA + ICLR_LORA + GATE_LORA
RW_SPLITS = (RWKV_WIDTH, RWKV_WIDTH + DECAY_LORA, 2 * RWKV_WIDTH + DECAY_LORA,
             3 * RWKV_WIDTH + DECAY_LORA, 3 * RWKV_WIDTH + DECAY_LORA + ICLR_LORA)
CMP_BLOCK = 32
CMP_STRIDE = 16
CMP_HIDDEN = 256
SEL_BLOCK = 64
SEL_TOPK = 8
N_LOCAL_BLOCKS = 2
WINDOW = 256
D_FF = 2752
CONV_WIDTH = 3
ROPE_THETA = 10000.0
NORM_EPS = 1e-6
GN_EPS = 64e-5
NEG_BIG = -1e30

LANES = 128
SUBLANES = 8
D_FF_PAD = -(-D_FF // LANES) * LANES
GATE_PAD = NSA_KV_HEADS * LANES
COL_Q = 0
COL_KV = NSA_WIDTH
COL_RW = COL_KV + 6 * KV_WIDTH
COL_G = COL_RW + RW_COLS
IN_COLS_PAD = COL_G + GATE_PAD
TM_PROMPT = 256
KEY_TILE = 256
WIN_TILE = 128
QUERY_TILE = 128
VMEM_LIMIT = 56 * 1024 * 1024


def _cparams(*sem):
    return pltpu.CompilerParams(dimension_semantics=sem, vmem_limit_bytes=VMEM_LIMIT)


def _const_spec(shape):
    nd = len(shape)
    return pl.BlockSpec(shape, lambda *_: (0,) * nd, pipeline_mode=pl.Buffered(1))


def _rms(x, g):
    return x * lax.rsqrt(jnp.mean(x * x, axis=-1, keepdims=True) + NORM_EPS) * g


def _rope_pair(x, cos, sin):
    lane = lax.broadcasted_iota(jnp.int32, x.shape, 1)
    first = (lane % HEAD_DIM) < (HEAD_DIM // 2)
    swapped = jnp.where(first, pltpu.roll(x, LANES - HEAD_DIM // 2, 1), pltpu.roll(x, HEAD_DIM // 2, 1))
    return x * cos + swapped * sin


def _in_proj_kernel(prompt, x_ref, g_ref, w_ref, cos_ref, sin_ref, *outs):
    xn = _rms(x_ref[...], g_ref[...]).astype(BF16)
    proj = jnp.dot(xn, w_ref[...], preferred_element_type=F32)
    cos, sin = cos_ref[0], sin_ref[0]
    scale = HEAD_DIM ** -0.5
    q = jnp.concatenate(
        [_rope_pair(proj[:, COL_Q + c * LANES:COL_Q + (c + 1) * LANES], cos, sin) for c in range(NSA_WIDTH // LANES)],
        axis=1) * scale
    kv = []
    for c in range(6):
        blk = proj[:, COL_KV + c * KV_WIDTH:COL_KV + (c + 1) * KV_WIDTH]
        kv.append(_rope_pair(blk, cos, sin) if c % 2 == 0 else blk)
    kv = jnp.concatenate(kv, axis=1)
    if prompt:
        q_ref, kvt_ref, kvtb_ref, wint_ref, wintb_ref, cmpn_ref, gates_ref, rw_ref = outs
        tm = kv.shape[0]
        kvt = kv[:, :4 * KV_WIDTH].T
        kvt_ref[0] = kvt
        kvtb_ref[0, 0] = kvt.astype(BF16).reshape(4 * NSA_KV_HEADS, HEAD_DIM, tm)
        wint = kv[:, 4 * KV_WIDTH:].T
        wint_ref[0] = wint
        for j in range(tm // WIN_TILE):
            wintb_ref[0, j] = wint[:, j * WIN_TILE:(j + 1) * WIN_TILE].astype(BF16).reshape(
                2 * NSA_KV_HEADS, HEAD_DIM, WIN_TILE)
        cmpn_ref[0] = kv[:, :KV_WIDTH]
        cmpn_ref[1] = kv[:, KV_WIDTH:2 * KV_WIDTH]
    else:
        q_ref, kvn_ref, gates_ref, rw_ref = outs
        kvn_ref[...] = kv
    q_ref[...] = q.astype(BF16)
    gates_ref[...] = proj[:, COL_G:COL_G + GATE_PAD]
    rw_ref[...] = proj[:, COL_RW:COL_RW + RW_COLS]


def _in_proj(x, g, w, cos, sin, *, prompt, n_seq, tm):
    n, d = x.shape
    nt = n // tm
    n_tab = cos.shape[0]
    tps = nt // n_seq if prompt else 1
    row = lambda w_: pl.BlockSpec((tm, w_), lambda i: (i, 0))
    in_specs = [row(d), _const_spec((1, d)), _const_spec(w.shape),
                pl.BlockSpec((1,) + cos.shape[1:], lambda i: (i % n_tab, 0, 0)),
                pl.BlockSpec((1,) + sin.shape[1:], lambda i: (i % n_tab, 0, 0))]
    if prompt:
        t = n // n_seq
        out_shape = [
            jax.ShapeDtypeStruct((n, NSA_WIDTH), BF16),
            jax.ShapeDtypeStruct((n_seq, 4 * KV_WIDTH, t), F32),
            jax.ShapeDtypeStruct((n_seq, t // tm, 4 * NSA_KV_HEADS, HEAD_DIM, tm), BF16),
            jax.ShapeDtypeStruct((n_seq, 2 * KV_WIDTH, WINDOW), F32),
            jax.ShapeDtypeStruct((n_seq, t // WIN_TILE, 2 * NSA_KV_HEADS, HEAD_DIM, WIN_TILE), BF16),
            jax.ShapeDtypeStruct((2, n, KV_WIDTH), F32),
            jax.ShapeDtypeStruct((n, GATE_PAD), F32),
            jax.ShapeDtypeStruct((n, RW_COLS), F32),
        ]
        wt = tm // WIN_TILE
        out_specs = [
            row(NSA_WIDTH),
            pl.BlockSpec((1, 4 * KV_WIDTH, tm), lambda i: (i // tps, 0, i % tps)),
            pl.BlockSpec((1, 1, 4 * NSA_KV_HEADS, HEAD_DIM, tm), lambda i: (i // tps, i % tps, 0, 0, 0)),
            pl.BlockSpec((1, 2 * KV_WIDTH, WINDOW), lambda i: (i // tps, 0, 0)),
            pl.BlockSpec((1, wt, 2 * NSA_KV_HEADS, HEAD_DIM, WIN_TILE), lambda i: (i // tps, i % tps, 0, 0, 0)),
            pl.BlockSpec((2, tm, KV_WIDTH), lambda i: (0, i, 0)), row(GATE_PAD), row(RW_COLS),
        ]
    else:
        out_shape = [
            jax.ShapeDtypeStruct((n, NSA_WIDTH), BF16),
            jax.ShapeDtypeStruct((n, 6 * KV_WIDTH), F32),
            jax.ShapeDtypeStruct((n, GATE_PAD), F32),
            jax.ShapeDtypeStruct((n, RW_COLS), F32),
        ]
        out_specs = [row(NSA_WIDTH), row(6 * KV_WIDTH), row(GATE_PAD), row(RW_COLS)]
    return pl.pallas_call(
        functools.partial(_in_proj_kernel, prompt),
        grid=(nt,), in_specs=in_specs, out_specs=out_specs, out_shape=out_shape,
        compiler_params=_cparams("arbitrary"), name="in_proj_prompt" if prompt else "in_proj_sample",
    )(x, g, w, cos, sin)


def _ffn_body(x_ref, on_ref, orw_ref, wo_ref, g_ref, wg_ref, wu_ref):
    mix = jnp.concatenate([on_ref[...], orw_ref[...]], axis=1)
    x1 = x_ref[...] + jnp.dot(mix, wo_ref[...], preferred_element_type=F32)
    xn = _rms(x1, g_ref[...]).astype(BF16)
    gate = jnp.dot(xn, wg_ref[...], preferred_element_type=F32)
    up = jnp.dot(xn, wu_ref[...], preferred_element_type=F32)
    return x1, gate, up


def _ffn_tail(x1, conv, up, wd_ref, xo_ref):
    h = (jax.nn.silu(conv) * up).astype(BF16)
    xo_ref[...] = x1 + jnp.dot(h, wd_ref[...], preferred_element_type=F32)


def _ffn_prompt_kernel(tps, x_ref, on_ref, orw_ref, wo_ref, g_ref, wg_ref, wu_ref, cw_ref, cb_ref, wd_ref,
                       xo_ref, conv_ref, ext_ref):
    tm = x_ref.shape[0]
    lo = SUBLANES - (CONV_WIDTH - 1)

    @pl.when(pl.program_id(0) % tps == 0)
    def _():
        ext_ref[0:SUBLANES, :] = jnp.zeros((SUBLANES, ext_ref.shape[1]), F32)

    x1, gate, up = _ffn_body(x_ref, on_ref, orw_ref, wo_ref, g_ref, wg_ref, wu_ref)
    ext_ref[SUBLANES:SUBLANES + tm, :] = gate
    conv = cb_ref[...] + sum(cw_ref[j:j + 1, :] * ext_ref[lo + j:lo + j + tm, :] for j in range(CONV_WIDTH))
    last = ext_ref[tm + lo:tm + SUBLANES, :]
    conv_ref[0] = last
    ext_ref[lo:SUBLANES, :] = last
    _ffn_tail(x1, conv, up, wd_ref, xo_ref)


def _ffn_sample_kernel(x_ref, on_ref, orw_ref, wo_ref, g_ref, wg_ref, wu_ref, cw_ref, cb_ref, wd_ref, st_ref,
                       xo_ref, conv_ref, carry_ref):
    @pl.when(pl.program_id(0) == 0)
    def _():
        carry_ref[...] = st_ref[...]

    x1, gate, up = _ffn_body(x_ref, on_ref, orw_ref, wo_ref, g_ref, wg_ref, wu_ref)
    prev = carry_ref[1]
    conv = cb_ref[...] + cw_ref[0:1, :] * carry_ref[0] + cw_ref[1:2, :] * prev + cw_ref[2:3, :] * gate
    carry_ref[0] = prev
    carry_ref[1] = gate
    conv_ref[0] = prev
    conv_ref[1] = gate
    _ffn_tail(x1, conv, up, wd_ref, xo_ref)


def _ffn(x, o_nsa, o_rw, wo, g, wg, wu, cw, cb, wd, *, n_seq, tm, conv_state=None):
    n, d = x.shape
    nt = n // tm
    f = wg.shape[1]
    row = lambda w_: pl.BlockSpec((tm, w_), lambda i: (i, 0))
    in_specs = [row(d), row(NSA_WIDTH), row(RWKV_WIDTH), _const_spec(wo.shape), _const_spec((1, d)),
                _const_spec(wg.shape), _const_spec(wu.shape), _const_spec(cw.shape), _const_spec(cb.shape),
                _const_spec(wd.shape)]
    args = [x, o_nsa, o_rw, wo, g, wg, wu, cw, cb, wd]
    if conv_state is None:
        tps = nt // n_seq
        kern = functools.partial(_ffn_prompt_kernel, tps)
        out_shape = [jax.ShapeDtypeStruct((n, d), F32), jax.ShapeDtypeStruct((n_seq, CONV_WIDTH - 1, f), F32)]
        out_specs = [row(d), pl.BlockSpec((1, CONV_WIDTH - 1, f), lambda i: (i // tps, 0, 0))]
        scratch = [pltpu.VMEM((tm + SUBLANES, f), F32)]
        name = "ffn_prompt"
    else:
        kern = _ffn_sample_kernel
        in_specs.append(_const_spec(conv_state.shape))
        args.append(conv_state)
        out_shape = [jax.ShapeDtypeStruct((n, d), F32), jax.ShapeDtypeStruct((CONV_WIDTH - 1, tm, f), F32)]
        out_specs = [row(d), pl.BlockSpec((CONV_WIDTH - 1, tm, f), lambda i: (0, 0, 0))]
        scratch = [pltpu.VMEM((CONV_WIDTH - 1, tm, f), F32)]
        name = "ffn_sample"
    return pl.pallas_call(
        kern, grid=(nt,), in_specs=in_specs, out_specs=out_specs, out_shape=out_shape, scratch_shapes=scratch,
        compiler_params=_cparams("arbitrary"), name=name,
    )(*args)


def _final_norm_kernel(x_ref, g_ref, o_ref):
    o_ref[...] = _rms(x_ref[...], g_ref[...])


def _final_norm(x, g, tm):
    n, d = x.shape
    return pl.pallas_call(
        _final_norm_kernel, grid=(n // tm,),
        in_specs=[pl.BlockSpec((tm, d), lambda i: (i, 0)), _const_spec((1, d))],
        out_specs=pl.BlockSpec((tm, d), lambda i: (i, 0)),
        out_shape=jax.ShapeDtypeStruct((n, d), F32),
        compiler_params=_cparams("arbitrary"), name="final_norm",
    )(x, g)


_NT = (((1,), (1,)), ((), ()))


def _iota(shape, dim):
    return lax.broadcasted_iota(jnp.int32, shape, dim)


def _softmax_masked(s, mask):
    s = jnp.where(mask, s, NEG_BIG)
    m = jnp.max(s, axis=-1, keepdims=True)
    e = jnp.where(mask, jnp.exp(s - m), 0.0)
    return e / jnp.maximum(jnp.sum(e, axis=-1, keepdims=True), 1e-30)


def _online(carry, s, mask, pv):
    m, l, acc = carry
    s = jnp.where(mask, s, NEG_BIG)
    m_new = jnp.maximum(m, jnp.max(s, axis=-1, keepdims=True))
    alpha = jnp.exp(m - m_new)
    p = jnp.where(mask, jnp.exp(s - m_new), 0.0)
    return m_new, alpha * l + jnp.sum(p, axis=-1, keepdims=True), alpha * acc + pv(p.astype(BF16))


def _online_init(rows):
    return (jnp.full((rows, 1), NEG_BIG, F32), jnp.zeros((rows, 1), F32), jnp.zeros((rows, HEAD_DIM), F32))


def _importance(p_sum, ov):
    hi = p_sum.astype(BF16)
    lo = (p_sum - hi.astype(F32)).astype(BF16)
    return jnp.dot(hi, ov, preferred_element_type=F32) + jnp.dot(lo, ov, preferred_element_type=F32)


def _select_blocks(imp, qpos, n_sel):
    lane = _iota(imp.shape, 1)
    cur = qpos // SEL_BLOCK
    forced = (lane == 0) | ((lane <= cur) & (lane > cur - N_LOCAL_BLOCKS))
    future = lane * SEL_BLOCK > qpos
    score = jnp.where(forced, 1e9, jnp.where(future, -1e9, imp))
    rank = jnp.zeros(imp.shape, F32)
    for sp in range(n_sel):
        col = score[:, sp:sp + 1]
        rank = rank + ((col > score) | ((col == score) & (lane > sp))).astype(F32)
    return ((rank < min(SEL_TOPK, n_sel)) & (lane < n_sel)).astype(F32)


def _gate_cols(gl, branch, reps):
    return jnp.concatenate([gl[:, r * 3 + branch:r * 3 + branch + 1] for r in range(reps)], axis=0)


def _cmp_const_kernel(pe_ref, w1_ref, c_ref):
    for kind in range(2):
        c_ref[kind] = jnp.dot(pe_ref[kind].astype(BF16), w1_ref[kind], preferred_element_type=F32)


def _cmp_const(pe_flat, w1):
    return pl.pallas_call(
        _cmp_const_kernel, grid=(1,),
        in_specs=[_const_spec(pe_flat.shape), _const_spec(w1.shape)],
        out_specs=pl.BlockSpec((2, SUBLANES, CMP_HIDDEN), lambda i: (0, 0, 0)),
        out_shape=jax.ShapeDtypeStruct((2, SUBLANES, CMP_HIDDEN), F32),
        compiler_params=_cparams("arbitrary"), name="cmp_const",
    )(pe_flat, w1)


def _compress_tail(p, q, c_ref, w2_ref, kind):
    h = p + pltpu.roll(q, q.shape[0] - 1, 0) + c_ref[kind, 0:1, :]
    return jnp.dot(jax.nn.silu(h).astype(BF16), w2_ref[kind], preferred_element_type=F32).astype(BF16)


def _compress_prompt_kernel(x_ref, w1_ref, w2_ref, c_ref, o_ref):
    nch = x_ref.shape[1] // CMP_STRIDE
    acc = [[jnp.zeros((nch, 2 * CMP_HIDDEN), F32) for _ in range(NSA_KV_HEADS)] for _ in range(2)]
    for j in range(CMP_STRIDE):
        for kind in range(2):
            xj = x_ref[kind, pl.ds(j, nch, stride=CMP_STRIDE), :].astype(BF16)
            for g in range(NSA_KV_HEADS):
                acc[kind][g] = acc[kind][g] + jnp.dot(xj[:, g * HEAD_DIM:(g + 1) * HEAD_DIM], w1_ref[kind, j],
                                                      preferred_element_type=F32)
    for kind in range(2):
        for g in range(NSA_KV_HEADS):
            a = acc[kind][g]
            o_ref[0, kind, g] = _compress_tail(a[:, :CMP_HIDDEN], a[:, CMP_HIDDEN:], c_ref, w2_ref, kind)


def _compress_prompt(cmpn, w1r, w2, cconst, n_seq):
    n = cmpn.shape[1]
    t = n // n_seq
    nch = t // CMP_STRIDE
    return pl.pallas_call(
        _compress_prompt_kernel, grid=(n_seq,),
        in_specs=[pl.BlockSpec((2, t, KV_WIDTH), lambda b: (0, b, 0)), _const_spec(w1r.shape), _const_spec(w2.shape),
                  _const_spec(cconst.shape)],
        out_specs=pl.BlockSpec((1, 2, NSA_KV_HEADS, nch, HEAD_DIM), lambda b: (b, 0, 0, 0, 0)),
        out_shape=jax.ShapeDtypeStruct((n_seq, 2, NSA_KV_HEADS, nch, HEAD_DIM), BF16),
        compiler_params=_cparams("arbitrary"), name="compress_prompt",
    )(cmpn, w1r, w2, cconst)


def _nsa_prompt_kernel(n_sel, q_ref, kc_ref, vc_ref, ks_ref, vs_ref, kw_ref, vw_ref, gates_ref, ov_ref, o_ref):
    qi = pl.program_id(2)
    nq, reps = QUERY_TILE, NSA_GROUP
    rows = nq * reps
    q0 = qi * nq
    qblk = q_ref[...]
    qrows = jnp.concatenate([qblk[:, r * HEAD_DIM:(r + 1) * HEAD_DIM] for r in range(reps)], axis=0)
    qpos1 = q0 + _iota((nq, 1), 0)
    qpos = jnp.concatenate([qpos1] * reps, axis=0)

    kc, vc = kc_ref[0, 0, 0], vc_ref[0, 0, 0]
    s_c = lax.dot_general(qrows, kc, _NT, preferred_element_type=F32)
    cidx = _iota((1, kc.shape[0]), 1)
    p_c = _softmax_masked(s_c, cidx * CMP_STRIDE + (CMP_BLOCK - 1) <= qpos)
    o_c = jnp.dot(p_c.astype(BF16), vc, preferred_element_type=F32)
    p_sum = p_c[0:nq]
    for r in range(1, reps):
        p_sum = p_sum + p_c[r * nq:(r + 1) * nq]
    sel = _select_blocks(_importance(p_sum, ov_ref[...]), qpos1, n_sel).astype(BF16)

    def sel_body(kt, carry):
        kt_t, vt_t = ks_ref[0, kt, 0], vs_ref[0, kt, 0]
        s = jnp.dot(qrows, kt_t, preferred_element_type=F32)
        kpos = kt * KEY_TILE + _iota((1, KEY_TILE), 1)
        expand = (_iota((LANES, KEY_TILE), 0) == kpos // SEL_BLOCK).astype(BF16)
        picked = jnp.dot(sel, expand, preferred_element_type=F32) > 0.5
        mask = jnp.concatenate([picked] * reps, axis=0) & (kpos <= qpos)
        return _online(carry, s, mask, lambda p: lax.dot_general(p, vt_t, _NT, preferred_element_type=F32))

    _, l_s, acc_s = lax.fori_loop(0, (q0 + nq - 1) // KEY_TILE + 1, sel_body, _online_init(rows))
    o_s = acc_s / jnp.maximum(l_s, 1e-30)

    carry = _online_init(rows)
    n_back = WINDOW // WIN_TILE
    for d in range(n_back + 1):
        j = qi - n_back + d
        jc = jnp.maximum(j, 0)
        kt_t, vt_t = kw_ref[0, jc, 0], vw_ref[0, jc, 0]
        s = jnp.dot(qrows, kt_t, preferred_element_type=F32)
        wpos = j * WIN_TILE + _iota((1, WIN_TILE), 1)
        mask = (wpos >= 0) & (wpos <= qpos) & (wpos > qpos - WINDOW)
        carry = _online(carry, s, mask,
                        lambda p, vt_t=vt_t: lax.dot_general(p, vt_t, _NT, preferred_element_type=F32))
    o_w = carry[2] / jnp.maximum(carry[1], 1e-30)

    gl = jax.nn.sigmoid(gates_ref[...])
    o = _gate_cols(gl, 0, reps) * o_c + _gate_cols(gl, 1, reps) * o_s + _gate_cols(gl, 2, reps) * o_w
    o_ref[...] = jnp.concatenate([o[r * nq:(r + 1) * nq] for r in range(reps)], axis=1).astype(BF16)


def _nsa_prompt(q, kc, kvtb, wintb, gates, ov, n_seq):
    n = q.shape[0]
    t = n // n_seq
    nqt = t // QUERY_TILE
    nkt, nwt = kvtb.shape[1], wintb.shape[1]
    nch = kc.shape[3]
    g2 = NSA_KV_HEADS
    qrow = lambda w_: pl.BlockSpec((QUERY_TILE, w_), lambda b, g, i: (b * nqt + i, g))
    cmp_spec = lambda kind: pl.BlockSpec((1, 1, 1, nch, HEAD_DIM), lambda b, g, i: (b, kind, g, 0, 0))
    sel_spec = lambda kind: pl.BlockSpec((1, nkt, 1, HEAD_DIM, KEY_TILE), lambda b, g, i: (b, 0, kind * g2 + g, 0, 0))
    win_spec = lambda kind: pl.BlockSpec((1, nwt, 1, HEAD_DIM, WIN_TILE), lambda b, g, i: (b, 0, kind * g2 + g, 0, 0))
    return pl.pallas_call(
        functools.partial(_nsa_prompt_kernel, t // SEL_BLOCK),
        grid=(n_seq, g2, nqt),
        in_specs=[qrow(NSA_GROUP * HEAD_DIM), cmp_spec(0), cmp_spec(1), sel_spec(2), sel_spec(3),
                  win_spec(0), win_spec(1), qrow(LANES), _const_spec(ov.shape)],
        out_specs=qrow(NSA_GROUP * HEAD_DIM),
        out_shape=jax.ShapeDtypeStruct((n, NSA_WIDTH), BF16),
        compiler_params=_cparams("arbitrary", "arbitrary", "arbitrary"), name="nsa_prompt",
    )(q, kc, kc, kvtb, kvtb, wintb, wintb, gates, ov)


def _nsa_sample_kernel(n_pages, n_sel, pt_ref, q_ref, kvn_ref, gates_ref, win_ref, w1_ref, w2_ref, c_ref, ov_ref,
                       ex_ref, *rest):
    del pt_ref
    pages, o_ref = rest[:n_pages], rest[n_pages]
    xt_ref, kst_ref, vst_ref = rest[n_pages + 1:]
    past = n_pages * PAGE_SIZE
    nch = past // CMP_STRIDE
    for p in range(n_pages):
        lo, hi = p * PAGE_SIZE, (p + 1) * PAGE_SIZE
        xt_ref[0, lo:hi, :] = pages[p][0, 0, 0:KV_WIDTH, :].T
        xt_ref[1, lo:hi, :] = pages[p][0, 0, KV_WIDTH:2 * KV_WIDTH, :].T
        kst_ref[:, lo:hi] = pages[p][0, 0, 2 * KV_WIDTH:3 * KV_WIDTH, :].astype(BF16)
        vst_ref[:, lo:hi] = pages[p][0, 0, 3 * KV_WIDTH:4 * KV_WIDTH, :].astype(BF16)

    cmp = [[None] * NSA_KV_HEADS for _ in range(2)]
    for kind in range(2):
        acc = jnp.zeros((nch, 2 * NSA_KV_HEADS * CMP_HIDDEN), F32)
        for jp in range(CMP_STRIDE // 2):
            a0 = xt_ref[kind, pl.ds(2 * jp, nch, stride=CMP_STRIDE), :]
            a1 = xt_ref[kind, pl.ds(2 * jp + 1, nch, stride=CMP_STRIDE), :]
            lhs = jnp.concatenate([a0, a1], axis=1).astype(BF16)
            acc = acc + jnp.dot(lhs, w1_ref[kind, jp], preferred_element_type=F32)
        for g in range(NSA_KV_HEADS):
            base = g * 2 * CMP_HIDDEN
            cmp[kind][g] = _compress_tail(acc[:, base:base + CMP_HIDDEN], acc[:, base + CMP_HIDDEN:base + 2 * CMP_HIDDEN],
                                          c_ref, w2_ref, kind)

    nt, reps = q_ref.shape[1], NSA_GROUP
    rows = nt * reps
    qb = q_ref[0]
    kvn = kvn_ref[0]
    tq1 = _iota((nt, 1), 0)
    tq = jnp.concatenate([tq1] * reps, axis=0)
    tq2 = jnp.concatenate([tq] * NSA_KV_HEADS, axis=0)
    tk = _iota((1, nt), 1)

    def qrows(g):
        return jnp.concatenate([qb[:, (g * reps + r) * HEAD_DIM:(g * reps + r + 1) * HEAD_DIM] for r in range(reps)],
                               axis=0)

    o_c, sels = [], []
    cidx = _iota((1, nch), 1)
    for g in range(NSA_KV_HEADS):
        s_c = lax.dot_general(qrows(g), cmp[0][g], _NT, preferred_element_type=F32)
        p_c = _softmax_masked(s_c, cidx * CMP_STRIDE + (CMP_BLOCK - 1) <= past + tq)
        o_c.append(jnp.dot(p_c.astype(BF16), cmp[1][g], preferred_element_type=F32))
        p_sum = p_c[0:nt]
        for r in range(1, reps):
            p_sum = p_sum + p_c[r * nt:(r + 1) * nt]
        sels.append(_select_blocks(_importance(p_sum, ov_ref[...]), past + tq1, n_sel))

    zeros = jnp.zeros((rows, HEAD_DIM), BF16)
    qbd = jnp.concatenate([jnp.concatenate([qrows(0), zeros], axis=1),
                           jnp.concatenate([zeros, qrows(1)], axis=1)], axis=0)

    def two_group_attention(s_past, mask_past, k_new, v_new, mask_new, vt_past):
        s_new = lax.dot_general(qbd, k_new, _NT, preferred_element_type=F32)
        sp = jnp.where(mask_past, s_past, NEG_BIG)
        sn = jnp.where(mask_new, s_new, NEG_BIG)
        m = jnp.maximum(jnp.max(sp, axis=-1, keepdims=True), jnp.max(sn, axis=-1, keepdims=True))
        pp = jnp.where(mask_past, jnp.exp(sp - m), 0.0)
        pn = jnp.where(mask_new, jnp.exp(sn - m), 0.0)
        den = jnp.sum(pp, axis=-1, keepdims=True) + jnp.sum(pn, axis=-1, keepdims=True)
        o2 = (lax.dot_general(pp.astype(BF16), vt_past, _NT, preferred_element_type=F32)
              + jnp.dot(pn.astype(BF16), v_new, preferred_element_type=F32)) / jnp.maximum(den, 1e-30)
        return [o2[0:rows, 0:HEAD_DIM], o2[rows:2 * rows, HEAD_DIM:2 * HEAD_DIM]]

    causal_new = tk <= tq2
    nb = past // SEL_BLOCK
    sel2 = jnp.concatenate(sels, axis=0).astype(BF16)
    picked = jnp.dot(sel2, ex_ref[...], preferred_element_type=F32) > 0.5
    mask_past = jnp.concatenate([picked[0:nt]] * reps + [picked[nt:2 * nt]] * reps, axis=0)
    new_on = jnp.concatenate([sels[g][:, nb:nb + 1] for g in range(NSA_KV_HEADS) for _ in range(reps)], axis=0) > 0.5
    o_s = two_group_attention(
        jnp.dot(qbd, kst_ref[...], preferred_element_type=F32), mask_past,
        kvn[:, 2 * KV_WIDTH:3 * KV_WIDTH].astype(BF16), kvn[:, 3 * KV_WIDTH:4 * KV_WIDTH].astype(BF16),
        new_on & causal_new, vst_ref[...])
    kwt = win_ref[0, 0, 0:KV_WIDTH, :].astype(BF16)
    vwt = win_ref[0, 0, KV_WIDTH:2 * KV_WIDTH, :].astype(BF16)
    o_w = two_group_attention(
        jnp.dot(qbd, kwt, preferred_element_type=F32), _iota((1, WINDOW), 1) > tq2,
        kvn[:, 4 * KV_WIDTH:5 * KV_WIDTH].astype(BF16), kvn[:, 5 * KV_WIDTH:6 * KV_WIDTH].astype(BF16),
        causal_new, vwt)

    gl = jax.nn.sigmoid(gates_ref[0])
    heads = []
    for g in range(NSA_KV_HEADS):
        gg = gl[:, g * LANES:(g + 1) * LANES]
        o = _gate_cols(gg, 0, reps) * o_c[g] + _gate_cols(gg, 1, reps) * o_s[g] + _gate_cols(gg, 2, reps) * o_w[g]
        heads += [o[r * nt:(r + 1) * nt] for r in range(reps)]
    o_ref[0] = jnp.concatenate(heads, axis=1).astype(BF16)


def _nsa_sample(layer, page_table, cache_t, q_b, kvn_b, gates_b, win_t, w1p, w2, cconst, ov, ex):
    bd, nt, _ = q_b.shape
    n_pages = page_table.shape[1]
    past = n_pages * PAGE_SIZE
    n_sel = -(-(past + nt) // SEL_BLOCK)
    seq = lambda w_: pl.BlockSpec((1, nt, w_), lambda s, pt: (s, 0, 0))
    page_spec = lambda p: pl.BlockSpec((1, 1, 4 * KV_WIDTH, PAGE_SIZE), lambda s, pt: (layer, pt[s, p], 0, 0))
    grid_spec = pltpu.PrefetchScalarGridSpec(
        num_scalar_prefetch=1, grid=(bd,),
        in_specs=[seq(NSA_WIDTH), seq(6 * KV_WIDTH), seq(GATE_PAD),
                  pl.BlockSpec((1, 1, 2 * KV_WIDTH, WINDOW), lambda s, pt: (layer, s, 0, 0)),
                  _const_spec(w1p.shape), _const_spec(w2.shape), _const_spec(cconst.shape), _const_spec(ov.shape),
                  _const_spec(ex.shape)] + [page_spec(p) for p in range(n_pages)],
        out_specs=seq(NSA_WIDTH),
        scratch_shapes=[pltpu.VMEM((2, past, KV_WIDTH), F32), pltpu.VMEM((KV_WIDTH, past), BF16),
                        pltpu.VMEM((KV_WIDTH, past), BF16)])
    return pl.pallas_call(
        functools.partial(_nsa_sample_kernel, n_pages, n_sel), grid_spec=grid_spec,
        out_shape=jax.ShapeDtypeStruct((bd, nt, NSA_WIDTH), BF16),
        compiler_params=_cparams("arbitrary"), name="nsa_sample",
    )(page_table, q_b, kvn_b, gates_b, win_t, w1p, w2, cconst, ov, ex, *([cache_t] * n_pages))


def _wkv_kernel(w_ref, a_ref, b_ref, k_ref, r_ref, v_ref, s0_ref, y_ref, sf_ref, s_ref):
    c = pl.program_id(1)
    tc = w_ref.shape[1]
    nj = s_ref.shape[0]
    zero = jnp.zeros(s_ref.shape[1:], F32)

    @pl.when(c == 0)
    def _():
        s_ref[...] = s0_ref[0]

    sa0 = zero
    for j in range(nj):
        sa0 = sa0 + s_ref[j] * a_ref[0, 0, j:j + 1, :]

    def step(t, sa):
        tn = jnp.minimum(t + 1, tc - 1)
        v_t = v_ref[0, t]
        y, sa_next = zero, zero
        for j in range(nj):
            s = s_ref[j] * w_ref[0, t, j:j + 1, :] + sa * b_ref[0, t, j:j + 1, :] + v_t * k_ref[0, t, j:j + 1, :]
            s_ref[j] = s
            y = y + s * r_ref[0, t, j:j + 1, :]
            sa_next = sa_next + s * a_ref[0, tn, j:j + 1, :]
        y_ref[0, t] = y
        return sa_next

    lax.fori_loop(0, tc, step, sa0)

    @pl.when(c == pl.num_programs(1) - 1)
    def _():
        sf_ref[0] = s_ref[...]


def _wkv(w, a, b, k, r, v, s0, tc):
    g, t, nj, lanes = w.shape
    ni = v.shape[2]
    byj = pl.BlockSpec((1, tc, nj, lanes), lambda gi, c: (gi, c, 0, 0))
    byi = pl.BlockSpec((1, tc, ni, lanes), lambda gi, c: (gi, c, 0, 0))
    st = pl.BlockSpec((1, nj, ni, lanes), lambda gi, c: (gi, 0, 0, 0))
    return pl.pallas_call(
        _wkv_kernel, grid=(g, t // tc),
        in_specs=[byj] * 5 + [byi, st], out_specs=[byi, st],
        out_shape=[jax.ShapeDtypeStruct(v.shape, F32), jax.ShapeDtypeStruct(s0.shape, F32)],
        scratch_shapes=[pltpu.VMEM((nj, ni, lanes), F32)],
        compiler_params=_cparams("arbitrary", "arbitrary"), name="wkv_scan",
    )(w, a, b, k, r, v, s0)


def _heads(z):
    return z.reshape(z.shape[:-1] + (RWKV_HEADS, HEAD_DIM))


def _rwkv_features(p_rw, prev, lw):
    xm = p_rw + (prev - p_rw) * lw['rw_mu']
    r, wa, k, v, aa, ga = jnp.split(xm, list(RW_SPLITS), axis=-1)
    w = -jax.nn.softplus(-(lw['rw_w0'] + jnp.tanh(wa) @ lw['rw_w_b'])) - 0.5
    decay = jnp.exp(-jnp.exp(w))
    a = jax.nn.sigmoid(lw['rw_a0'] + aa @ lw['rw_a_b'])
    g = jax.nn.sigmoid(ga) @ lw['rw_g_b']
    kk = _heads(k * lw['rw_k_k'])
    kk = kk / jnp.maximum(jnp.sqrt(jnp.sum(kk * kk, axis=-1, keepdims=True)), 1e-12)
    a_h = _heads(a)
    k_h = _heads(k) * (1.0 + (a_h - 1.0) * lw['rw_k_a'].reshape(RWKV_HEADS, HEAD_DIM))
    return _heads(r), _heads(decay), k_h, _heads(v), -kk, kk * a_h, g


def _rwkv_out(y, r_h, k_h, v_h, g, lw):
    mu = jnp.mean(y, axis=-1, keepdims=True)
    var = jnp.mean(jnp.square(y - mu), axis=-1, keepdims=True)
    yn = ((y - mu) * lax.rsqrt(var + GN_EPS)).reshape(g.shape)
    yn = yn * lw['rw_gn_w'] + lw['rw_gn_b']
    bonus = jnp.sum(r_h * k_h * lw['rw_r_k'], axis=-1, keepdims=True) * v_h
    return (yn + bonus.reshape(g.shape)) * g


def _rope_tables(pos):
    half = HEAD_DIM // 2
    inv = ROPE_THETA ** (-jnp.arange(half, dtype=F32) / half)
    ang = pos.astype(F32)[:, None] * inv[None, :]
    cos, sin = jnp.cos(ang), jnp.sin(ang)
    reps = LANES // HEAD_DIM
    return jnp.tile(jnp.concatenate([cos, cos], axis=1), (1, reps)), jnp.tile(jnp.concatenate([-sin, sin], axis=1), (1, reps))


def _overlap(n_rows, n_cmp, n_sel):
    c = jnp.arange(n_rows)[:, None]
    s = jnp.arange(LANES)[None, :]
    hit = (c * CMP_STRIDE < (s + 1) * SEL_BLOCK) & (c * CMP_STRIDE + CMP_BLOCK - 1 >= s * SEL_BLOCK)
    return (hit & (c < n_cmp) & (s < n_sel)).astype(BF16)


def _prep_layer(l, p):
    w_in = p['w_in'][l]
    q_cols = w_in[:, :NSA_WIDTH]
    kv_cols = w_in[:, NSA_WIDTH:NSA_WIDTH + 6 * KV_WIDTH]
    g_cols = w_in[:, NSA_WIDTH + 6 * KV_WIDTH:NSA_WIDTH + 6 * KV_WIDTH + N_GATES]
    rw_cols = w_in[:, NSA_WIDTH + 6 * KV_WIDTH + N_GATES:]
    per_group = N_GATES // NSA_KV_HEADS
    g_pad = [jnp.pad(g_cols[:, g * per_group:(g + 1) * per_group], ((0, 0), (0, LANES - per_group)))
             for g in range(NSA_KV_HEADS)]
    fpad = D_FF_PAD - D_FF
    w1 = p['cmp_w1'][l].astype(BF16)
    w1s = w1.reshape(2, CMP_BLOCK, HEAD_DIM, CMP_HIDDEN)
    w1r = jnp.concatenate([w1s[:, :CMP_STRIDE], w1s[:, CMP_STRIDE:]], axis=-1)
    z = jnp.zeros_like(w1r[:, 0::2])
    w1p = jnp.concatenate([
        jnp.concatenate([w1r[:, 0::2], z], axis=-1), jnp.concatenate([z, w1r[:, 0::2]], axis=-1),
        jnp.concatenate([w1r[:, 1::2], z], axis=-1), jnp.concatenate([z, w1r[:, 1::2]], axis=-1)], axis=2)
    pe_flat = jnp.broadcast_to(p['cmp_pe'][l].reshape(2, 1, CMP_BLOCK * HEAD_DIM), (2, SUBLANES, CMP_BLOCK * HEAD_DIM))
    lw = {k: p[k][l] for k in ('rw_mu', 'rw_w0', 'rw_w_b', 'rw_a0', 'rw_a_b', 'rw_g_b', 'rw_k_k', 'rw_k_a', 'rw_r_k',
                               'rw_gn_w', 'rw_gn_b')}
    lw.update(
        norm_mix=p['norm_mix'][l][None], norm_ffn=p['norm_ffn'][l][None],
        w_in=jnp.concatenate([q_cols, kv_cols, rw_cols] + g_pad, axis=1).astype(BF16),
        w1r=w1r, w1p=w1p, w2=p['cmp_w2'][l].astype(BF16), cconst=_cmp_const(pe_flat, w1),
        w_out=p['w_out'][l].astype(BF16),
        wg=jnp.pad(p['ffn_w_gate'][l], ((0, 0), (0, fpad))).astype(BF16),
        wu=jnp.pad(p['ffn_w_up'][l], ((0, 0), (0, fpad))).astype(BF16),
        wd=jnp.pad(p['ffn_w_down'][l], ((0, fpad), (0, 0))).astype(BF16),
        cw=jnp.pad(p['ffn_conv_w'][l], ((0, 0), (0, fpad))), cb=jnp.pad(p['ffn_conv_b'][l], (0, fpad))[None])
    return lw


def _layer_prompt(x, lw, cos, sin, ov, b, t):
    n = b * t
    q, kvt, kvtb, wint, wintb, cmpn, gates, prw = _in_proj(
        x, lw['norm_mix'], lw['w_in'], cos, sin, prompt=True, n_seq=b, tm=TM_PROMPT)
    kc = _compress_prompt(cmpn, lw['w1r'], lw['w2'], lw['cconst'], b)
    o_nsa = _nsa_prompt(q, kc, kvtb, wintb, gates, ov, b)

    prw3 = prw.reshape(b, t, RW_COLS)
    prev = jnp.concatenate([jnp.zeros((b, 1, RW_COLS), F32), prw3[:, :-1]], axis=1)
    r_h, w_h, k_h, v_h, a_h, b_h, g = _rwkv_features(prw3, prev, lw)
    half = HEAD_DIM // 2
    lanes = 2 * b * RWKV_HEADS

    def by_key(z):
        zt = z.transpose(1, 3, 0, 2).reshape(t, HEAD_DIM, b * RWKV_HEADS)
        return jnp.concatenate([zt, zt], axis=-1)[None]

    v_p = v_h.reshape(b, t, RWKV_HEADS, 2, half).transpose(1, 4, 3, 0, 2).reshape(1, t, half, lanes)
    y_p, s_fin = _wkv(by_key(w_h), by_key(a_h), by_key(b_h), by_key(k_h), by_key(r_h), v_p,
                      jnp.zeros((1, HEAD_DIM, half, lanes), F32), tc=min(t, 32))
    y = y_p.reshape(t, half, 2, b, RWKV_HEADS).transpose(3, 0, 4, 2, 1).reshape(b, t, RWKV_HEADS, HEAD_DIM)
    wkv_new = s_fin.reshape(HEAD_DIM, half, 2, b, RWKV_HEADS).transpose(3, 4, 2, 1, 0).reshape(
        b, RWKV_HEADS, HEAD_DIM, HEAD_DIM)
    o_rw = _rwkv_out(y, r_h, k_h, v_h, g, lw).reshape(n, RWKV_WIDTH).astype(BF16)

    x_new, conv_new = _ffn(x, o_nsa, o_rw, lw['w_out'], lw['norm_ffn'], lw['wg'], lw['wu'], lw['cw'], lw['cb'],
                           lw['wd'], n_seq=b, tm=TM_PROMPT)
    rows = kvt.reshape(b, 4, NSA_KV_HEADS, HEAD_DIM, t).transpose(0, 4, 1, 2, 3)
    win = wint.reshape(b, 2, NSA_KV_HEADS, HEAD_DIM, WINDOW).transpose(0, 4, 1, 2, 3)
    return x_new, rows, win, wkv_new, prw3[:, -1], conv_new[:, :, :D_FF]


def _layer_sample(x, lw, l, cos, sin, ov, ex, page_table, cache_t, win_t, state_win, state_wkv, state_shift,
                  state_conv, bd, t):
    n = bd * t
    q, kvn, gates, prw = _in_proj(x, lw['norm_mix'], lw['w_in'], cos, sin, prompt=False, n_seq=bd, tm=bd)
    to_b = lambda z: z.reshape(t, bd, z.shape[-1]).transpose(1, 0, 2)
    kvn_b = to_b(kvn)
    o_b = _nsa_sample(l, page_table, cache_t, to_b(q), kvn_b, to_b(gates), win_t, lw['w1p'], lw['w2'], lw['cconst'],
                      ov, ex)
    o_nsa = o_b.transpose(1, 0, 2).reshape(n, NSA_WIDTH)

    prw3 = prw.reshape(t, bd, RW_COLS)
    prev = jnp.concatenate([state_shift[l][None], prw3[:-1]], axis=0)
    r_h, w_h, k_h, v_h, a_h, b_h, g = _rwkv_features(prw3, prev, lw)
    by_key = lambda z: z.transpose(2, 0, 3, 1)
    y_p, s_fin = _wkv(by_key(w_h), by_key(a_h), by_key(b_h), by_key(k_h), by_key(r_h), by_key(v_h),
                      state_wkv[l].transpose(1, 3, 2, 0), tc=t)
    y = y_p.transpose(1, 3, 0, 2)
    wkv_new = s_fin.transpose(3, 0, 2, 1)
    o_rw = _rwkv_out(y, r_h, k_h, v_h, g, lw).reshape(n, RWKV_WIDTH).astype(BF16)

    conv_state = jnp.pad(state_conv[l].transpose(1, 0, 2), ((0, 0), (0, 0), (0, D_FF_PAD - D_FF)))
    x_new, conv_new = _ffn(x, o_nsa, o_rw, lw['w_out'], lw['norm_ffn'], lw['wg'], lw['wu'], lw['cw'], lw['cb'],
                           lw['wd'], n_seq=bd, tm=bd, conv_state=conv_state)
    rows = kvn_b[:, :, :4 * KV_WIDTH].reshape(bd, t, 4, NSA_KV_HEADS, HEAD_DIM)
    win_rows = kvn_b[:, :, 4 * KV_WIDTH:].reshape(bd, t, 2, NSA_KV_HEADS, HEAD_DIM)
    win_valid = jnp.concatenate([state_win[l], win_rows], axis=1)
    win = win_valid[:, win_valid.shape[1] - min(WINDOW, win_valid.shape[1]):]
    return x_new, rows, win, wkv_new, prw3[-1], conv_new.transpose(1, 0, 2)[:, :, :D_FF]


def kernel(x_prompt, x_sample, cache_kv, page_table, state_win, state_wkv, state_shift, state_conv,
           norm_mix, w_in, cmp_pe, cmp_w1, cmp_w2, rw_mu, rw_w0, rw_w_b, rw_a0, rw_a_b, rw_g_b,
           rw_k_k, rw_k_a, rw_r_k, rw_gn_w, rw_gn_b, w_out, norm_ffn, ffn_w_gate, ffn_w_up,
           ffn_conv_w, ffn_conv_b, ffn_w_down, norm_final):
    p = dict(norm_mix=norm_mix, w_in=w_in, cmp_pe=cmp_pe, cmp_w1=cmp_w1, cmp_w2=cmp_w2, rw_mu=rw_mu, rw_w0=rw_w0,
             rw_w_b=rw_w_b, rw_a0=rw_a0, rw_a_b=rw_a_b, rw_g_b=rw_g_b, rw_k_k=rw_k_k, rw_k_a=rw_k_a, rw_r_k=rw_r_k,
             rw_gn_w=rw_gn_w, rw_gn_b=rw_gn_b, w_out=w_out, norm_ffn=norm_ffn, ffn_w_gate=ffn_w_gate,
             ffn_w_up=ffn_w_up, ffn_conv_w=ffn_conv_w, ffn_conv_b=ffn_conv_b, ffn_w_down=ffn_w_down)
    depth = w_in.shape[0]
    b, t, d = x_prompt.shape
    bd, ts, _ = x_sample.shape
    n_pages = page_table.shape[1]
    past = n_pages * PAGE_SIZE
    n_phys = cache_kv.shape[1]
    assert d == D_MODEL and t % TM_PROMPT == 0 and t >= WINDOW and TM_PROMPT == WINDOW
    assert state_win.shape[2] == WINDOW and past >= WINDOW and past % SEL_BLOCK == 0 and ts <= SUBLANES
    assert bd % SUBLANES == 0 and cache_kv.shape[2] == PAGE_SIZE
    n_cmp_s = (past + ts - CMP_BLOCK) // CMP_STRIDE + 1
    assert (n_cmp_s - 1) * CMP_STRIDE + CMP_BLOCK <= past

    cos_p, sin_p = (z.reshape(t // TM_PROMPT, TM_PROMPT, LANES) for z in _rope_tables(jnp.arange(t)))
    cos_s, sin_s = (z.reshape(ts, 1, LANES) for z in _rope_tables(past + jnp.arange(ts)))
    ov_p = _overlap(t // CMP_STRIDE, (t - CMP_BLOCK) // CMP_STRIDE + 1, t // SEL_BLOCK)
    ov_s = _overlap(past // CMP_STRIDE, n_cmp_s, -(-(past + ts) // SEL_BLOCK))
    ex = (jnp.arange(LANES)[:, None] == (jnp.arange(past) // SEL_BLOCK)[None, :]).astype(BF16)
    cache_t = cache_kv.transpose(0, 1, 3, 4, 5, 2).reshape(depth, n_phys, 4 * KV_WIDTH, PAGE_SIZE)
    win_t = state_win.transpose(0, 1, 3, 4, 5, 2).reshape(depth, bd, 2 * KV_WIDTH, WINDOW)

    x_p = x_prompt.reshape(b * t, d)
    x_s = x_sample.transpose(1, 0, 2).reshape(ts * bd, d)
    outs = [[] for _ in range(10)]
    for l in range(depth):
        lw = _prep_layer(l, p)
        x_p, *res_p = _layer_prompt(x_p, lw, cos_p, sin_p, ov_p, b, t)
        x_s, *res_s = _layer_sample(x_s, lw, l, cos_s, sin_s, ov_s, ex, page_table, cache_t, win_t, state_win,
                                    state_wkv, state_shift, state_conv, bd, ts)
        for i in range(5):
            outs[2 * i].append(res_p[i])
            outs[2 * i + 1].append(res_s[i])
    g_fin = norm_final[None]
    y_prompt = _final_norm(x_p, g_fin, TM_PROMPT).reshape(b, t, d)
    y_sample = _final_norm(x_s, g_fin, bd).reshape(ts, bd, d).transpose(1, 0, 2)
    return (y_prompt, y_sample) + tuple(jnp.stack(o) for o in outs)
```

```python
import functools
import math

import jax
import jax.numpy as jnp
from jax import lax
from jax.experimental import pallas as pl
from jax.experimental.pallas import tpu as pltpu

F32 = jnp.float32
BF16 = jnp.bfloat16

D_MODEL = 1024
PAGE_SIZE = 128
HEAD_DIM = 64
NSA_HEADS = 8
NSA_KV_HEADS = 2
NSA_GROUP = NSA_HEADS // NSA_KV_HEADS
NSA_WIDTH = NSA_HEADS * HEAD_DIM
KV_WIDTH = NSA_KV_HEADS * HEAD_DIM
N_GATES = 3 * NSA_HEADS
RWKV_HEADS = 8
RWKV_WIDTH = RWKV_HEADS * HEAD_DIM
DECAY_LORA = 64
ICLR_LORA = 64
GATE_LORA = 128
RW_COLS = 3 * RWKV_WIDTH + DECAY_LORA + ICLR_LORA + GATE_LORA
RW_SPLITS = (RWKV_WIDTH, RWKV_WIDTH + DECAY_LORA, 2 * RWKV_WIDTH + DECAY_LORA,
             3 * RWKV_WIDTH + DECAY_LORA, 3 * RWKV_WIDTH + DECAY_LORA + ICLR_LORA)
CMP_BLOCK = 32
CMP_STRIDE = 16
CMP_HIDDEN = 256
SEL_BLOCK = 64
SEL_TOPK = 8
N_LOCAL_BLOCKS = 2
WINDOW = 256
D_FF = 2752
CONV_WIDTH = 3
ROPE_THETA = 10000.0
NORM_EPS = 1e-6
GN_EPS = 64e-5
NEG_BIG = -1e30

SEL_SHIFT = SEL_BLOCK.bit_length() - 1
assert 1 << SEL_SHIFT == SEL_BLOCK
RW_LORA = 3 * RWKV_WIDTH
assert DECAY_LORA + ICLR_LORA == 128


def _rw_permute(z):
    return jnp.concatenate([z[..., :RW_SPLITS[0]], z[..., RW_SPLITS[1]:RW_SPLITS[3]],
                            z[..., RW_SPLITS[0]:RW_SPLITS[1]], z[..., RW_SPLITS[3]:]], axis=-1)


def _rw_unpermute(z):
    return jnp.concatenate([z[..., :RWKV_WIDTH], z[..., RW_LORA:RW_LORA + DECAY_LORA],
                            z[..., RWKV_WIDTH:RW_LORA], z[..., RW_LORA + DECAY_LORA:]], axis=-1)


LANES = 128
SUBLANES = 8
D_FF_PAD = -(-D_FF // LANES) * LANES
GATE_PAD = NSA_KV_HEADS * LANES
COL_Q = 0
COL_KV = NSA_WIDTH
COL_RW = COL_KV + 6 * KV_WIDTH
COL_G = COL_RW + RW_COLS
IN_COLS_PAD = COL_G + GATE_PAD
TM_PROMPT = 256
KEY_TILE = 256
WIN_TILE = 128
QUERY_TILE = 128
VMEM_LIMIT = 56 * 1024 * 1024


def _cparams(*sem):
    return pltpu.CompilerParams(dimension_semantics=sem, vmem_limit_bytes=VMEM_LIMIT)


def _const_spec(shape):
    nd = len(shape)
    return pl.BlockSpec(shape, lambda *_: (0,) * nd, pipeline_mode=pl.Buffered(1))


def _rms(x, g):
    return x * lax.rsqrt(jnp.mean(x * x, axis=-1, keepdims=True) + NORM_EPS) * g


def _rope_pair(x, cos, sin):
    lane = lax.broadcasted_iota(jnp.int32, x.shape, 1)
    first = (lane % HEAD_DIM) < (HEAD_DIM // 2)
    swapped = jnp.where(first, pltpu.roll(x, LANES - HEAD_DIM // 2, 1), pltpu.roll(x, HEAD_DIM // 2, 1))
    return x * cos + swapped * sin


def _in_proj_kernel(prompt, x_ref, g_ref, w_ref, cos_ref, sin_ref, *outs):
    xn = _rms(x_ref[...], g_ref[...]).astype(BF16)
    proj = jnp.dot(xn, w_ref[...], preferred_element_type=F32)
    cos, sin = cos_ref[0], sin_ref[0]
    scale = HEAD_DIM ** -0.5
    q = jnp.concatenate(
        [_rope_pair(proj[:, COL_Q + c * LANES:COL_Q + (c + 1) * LANES], cos, sin) for c in range(NSA_WIDTH // LANES)],
        axis=1) * scale
    kv = []
    for c in range(6):
        blk = proj[:, COL_KV + c * KV_WIDTH:COL_KV + (c + 1) * KV_WIDTH]
        kv.append(_rope_pair(blk, cos, sin) if c % 2 == 0 else blk)
    kv = jnp.concatenate(kv, axis=1)
    if prompt:
        q_ref, kvt_ref, kvtb_ref, wint_ref, wintb_ref, cmpn_ref, gates_ref, rw_ref = outs
        tm = kv.shape[0]
        kvt = kv[:, :4 * KV_WIDTH].T
        kvt_ref[0] = kvt
        kvtb_ref[0, 0] = kvt.astype(BF16).reshape(4 * NSA_KV_HEADS, HEAD_DIM, tm)
        wint = kv[:, 4 * KV_WIDTH:].T
        wint_ref[0] = wint
        for j in range(tm // WIN_TILE):
            wintb_ref[0, j] = wint[:, j * WIN_TILE:(j + 1) * WIN_TILE].astype(BF16).reshape(
                2 * NSA_KV_HEADS, HEAD_DIM, WIN_TILE)
        cmpn_ref[0] = kv[:, :KV_WIDTH]
        cmpn_ref[1] = kv[:, KV_WIDTH:2 * KV_WIDTH]
    else:
        q_ref, kvn_ref, gates_ref, rw_ref = outs
        kvn_ref[...] = kv
    q_ref[...] = q.astype(BF16)
    gates_ref[...] = proj[:, COL_G:COL_G + GATE_PAD]
    rw_ref[...] = proj[:, COL_RW:COL_RW + RW_COLS]


def _in_proj(x, g, w, cos, sin, *, prompt, n_seq, tm):
    n, d = x.shape
    nt = n // tm
    n_tab = cos.shape[0]
    tps = nt // n_seq if prompt else 1
    row = lambda w_: pl.BlockSpec((tm, w_), lambda i: (i, 0))
    in_specs = [row(d), _const_spec((1, d)), _const_spec(w.shape),
                pl.BlockSpec((1,) + cos.shape[1:], lambda i: (i % n_tab, 0, 0)),
                pl.BlockSpec((1,) + sin.shape[1:], lambda i: (i % n_tab, 0, 0))]
    if prompt:
        t = n // n_seq
        out_shape = [
            jax.ShapeDtypeStruct((n, NSA_WIDTH), BF16),
            jax.ShapeDtypeStruct((n_seq, 4 * KV_WIDTH, t), F32),
            jax.ShapeDtypeStruct((n_seq, t // tm, 4 * NSA_KV_HEADS, HEAD_DIM, tm), BF16),
            jax.ShapeDtypeStruct((n_seq, 2 * KV_WIDTH, WINDOW), F32),
            jax.ShapeDtypeStruct((n_seq, t // WIN_TILE, 2 * NSA_KV_HEADS, HEAD_DIM, WIN_TILE), BF16),
            jax.ShapeDtypeStruct((2, n, KV_WIDTH), F32),
            jax.ShapeDtypeStruct((n, GATE_PAD), F32),
            jax.ShapeDtypeStruct((n, RW_COLS), F32),
        ]
        wt = tm // WIN_TILE
        out_specs = [
            row(NSA_WIDTH),
            pl.BlockSpec((1, 4 * KV_WIDTH, tm), lambda i: (i // tps, 0, i % tps)),
            pl.BlockSpec((1, 1, 4 * NSA_KV_HEADS, HEAD_DIM, tm), lambda i: (i // tps, i % tps, 0, 0, 0)),
            pl.BlockSpec((1, 2 * KV_WIDTH, WINDOW), lambda i: (i // tps, 0, 0)),
            pl.BlockSpec((1, wt, 2 * NSA_KV_HEADS, HEAD_DIM, WIN_TILE), lambda i: (i // tps, i % tps, 0, 0, 0)),
            pl.BlockSpec((2, tm, KV_WIDTH), lambda i: (0, i, 0)), row(GATE_PAD), row(RW_COLS),
        ]
    else:
        out_shape = [
            jax.ShapeDtypeStruct((n, NSA_WIDTH), BF16),
            jax.ShapeDtypeStruct((n, 6 * KV_WIDTH), F32),
            jax.ShapeDtypeStruct((n, GATE_PAD), F32),
            jax.ShapeDtypeStruct((n, RW_COLS), F32),
        ]
        out_specs = [row(NSA_WIDTH), row(6 * KV_WIDTH), row(GATE_PAD), row(RW_COLS)]
    return pl.pallas_call(
        functools.partial(_in_proj_kernel, prompt),
        grid=(nt,), in_specs=in_specs, out_specs=out_specs, out_shape=out_shape,
        compiler_params=_cparams("arbitrary"), name="in_proj_prompt" if prompt else "in_proj_sample",
    )(x, g, w, cos, sin)


def _ffn_body(x_ref, on_ref, orw_ref, wo_ref, g_ref, wg_ref, wu_ref):
    mix = jnp.concatenate([on_ref[...], orw_ref[...]], axis=1)
    x1 = x_ref[...] + jnp.dot(mix, wo_ref[...], preferred_element_type=F32)
    xn = _rms(x1, g_ref[...]).astype(BF16)
    gate = jnp.dot(xn, wg_ref[...], preferred_element_type=F32)
    up = jnp.dot(xn, wu_ref[...], preferred_element_type=F32)
    return x1, gate, up


def _ffn_tail(x1, conv, up, wd_ref, xo_ref):
    h = (jax.nn.silu(conv) * up).astype(BF16)
    xo_ref[...] = x1 + jnp.dot(h, wd_ref[...], preferred_element_type=F32)


def _ffn_prompt_kernel(tps, x_ref, on_ref, orw_ref, wo_ref, g_ref, wg_ref, wu_ref, cw_ref, cb_ref, wd_ref,
                       xo_ref, conv_ref, ext_ref):
    tm = x_ref.shape[0]
    lo = SUBLANES - (CONV_WIDTH - 1)

    @pl.when(pl.program_id(0) % tps == 0)
    def _():
        ext_ref[0:SUBLANES, :] = jnp.zeros((SUBLANES, ext_ref.shape[1]), F32)

    x1, gate, up = _ffn_body(x_ref, on_ref, orw_ref, wo_ref, g_ref, wg_ref, wu_ref)
    ext_ref[SUBLANES:SUBLANES + tm, :] = gate
    conv = cb_ref[...] + sum(cw_ref[j:j + 1, :] * ext_ref[lo + j:lo + j + tm, :] for j in range(CONV_WIDTH))
    last = ext_ref[tm + lo:tm + SUBLANES, :]
    conv_ref[0] = last
    ext_ref[lo:SUBLANES, :] = last
    _ffn_tail(x1, conv, up, wd_ref, xo_ref)


def _ffn_sample_kernel(x_ref, on_ref, orw_ref, wo_ref, g_ref, wg_ref, wu_ref, cw_ref, cb_ref, wd_ref, st_ref,
                       xo_ref, conv_ref, carry_ref):
    @pl.when(pl.program_id(0) == 0)
    def _():
        carry_ref[...] = st_ref[...]

    x1, gate, up = _ffn_body(x_ref, on_ref, orw_ref, wo_ref, g_ref, wg_ref, wu_ref)
    prev = carry_ref[1]
    conv = cb_ref[...] + cw_ref[0:1, :] * carry_ref[0] + cw_ref[1:2, :] * prev + cw_ref[2:3, :] * gate
    carry_ref[0] = prev
    carry_ref[1] = gate
    conv_ref[0] = prev
    conv_ref[1] = gate
    _ffn_tail(x1, conv, up, wd_ref, xo_ref)


def _ffn(x, o_nsa, o_rw, wo, g, wg, wu, cw, cb, wd, *, n_seq, tm, conv_state=None):
    n, d = x.shape
    nt = n // tm
    f = wg.shape[1]
    row = lambda w_: pl.BlockSpec((tm, w_), lambda i: (i, 0))
    in_specs = [row(d), row(NSA_WIDTH), row(RWKV_WIDTH), _const_spec(wo.shape), _const_spec((1, d)),
                _const_spec(wg.shape), _const_spec(wu.shape), _const_spec(cw.shape), _const_spec(cb.shape),
                _const_spec(wd.shape)]
    args = [x, o_nsa, o_rw, wo, g, wg, wu, cw, cb, wd]
    if conv_state is None:
        tps = nt // n_seq
        kern = functools.partial(_ffn_prompt_kernel, tps)
        out_shape = [jax.ShapeDtypeStruct((n, d), F32), jax.ShapeDtypeStruct((n_seq, CONV_WIDTH - 1, f), F32)]
        out_specs = [row(d), pl.BlockSpec((1, CONV_WIDTH - 1, f), lambda i: (i // tps, 0, 0))]
        scratch = [pltpu.VMEM((tm + SUBLANES, f), F32)]
        name = "ffn_prompt"
    else:
        kern = _ffn_sample_kernel
        in_specs.append(_const_spec(conv_state.shape))
        args.append(conv_state)
        out_shape = [jax.ShapeDtypeStruct((n, d), F32), jax.ShapeDtypeStruct((CONV_WIDTH - 1, tm, f), F32)]
        out_specs = [row(d), pl.BlockSpec((CONV_WIDTH - 1, tm, f), lambda i: (0, 0, 0))]
        scratch = [pltpu.VMEM((CONV_WIDTH - 1, tm, f), F32)]
        name = "ffn_sample"
    return pl.pallas_call(
        kern, grid=(nt,), in_specs=in_specs, out_specs=out_specs, out_shape=out_shape, scratch_shapes=scratch,
        compiler_params=_cparams("arbitrary"), name=name,
    )(*args)


def _final_norm_kernel(x_ref, g_ref, o_ref):
    o_ref[...] = _rms(x_ref[...], g_ref[...])


def _final_norm(x, g, tm):
    n, d = x.shape
    return pl.pallas_call(
        _final_norm_kernel, grid=(n // tm,),
        in_specs=[pl.BlockSpec((tm, d), lambda i: (i, 0)), _const_spec((1, d))],
        out_specs=pl.BlockSpec((tm, d), lambda i: (i, 0)),
        out_shape=jax.ShapeDtypeStruct((n, d), F32),
        compiler_params=_cparams("arbitrary"), name="final_norm",
    )(x, g)


_NT = (((1,), (1,)), ((), ()))


def _iota(shape, dim):
    return lax.broadcasted_iota(jnp.int32, shape, dim)


def _softmax_masked(s, mask):
    s = jnp.where(mask, s, NEG_BIG)
    m = jnp.max(s, axis=-1, keepdims=True)
    e = jnp.where(mask, jnp.exp(s - m), 0.0)
    return e / jnp.maximum(jnp.sum(e, axis=-1, keepdims=True), 1e-30)


def _online(carry, s, mask, pv):
    m, l, acc = carry
    s = jnp.where(mask, s, NEG_BIG)
    m_new = jnp.maximum(m, jnp.max(s, axis=-1, keepdims=True))
    alpha = jnp.exp(m - m_new)
    p = jnp.where(mask, jnp.exp(s - m_new), 0.0)
    return m_new, alpha * l + jnp.sum(p, axis=-1, keepdims=True), alpha * acc + pv(p.astype(BF16))


def _online_init(rows):
    return (jnp.full((rows, 1), NEG_BIG, F32), jnp.zeros((rows, 1), F32), jnp.zeros((rows, HEAD_DIM), F32))


def _importance(p_sum, ov):
    hi = p_sum.astype(BF16)
    lo = (p_sum - hi.astype(F32)).astype(BF16)
    return jnp.dot(hi, ov, preferred_element_type=F32) + jnp.dot(lo, ov, preferred_element_type=F32)


def _select_blocks(imp, qpos, n_sel):
    lane = _iota(imp.shape, 1)
    cur = lax.shift_right_arithmetic(qpos, SEL_SHIFT)
    forced = (lane == 0) | ((lane <= cur) & (lane > cur - N_LOCAL_BLOCKS))
    future = lane * SEL_BLOCK > qpos
    score = jnp.where(forced, 1e9, jnp.where(future, -1e9, imp))
    rank = jnp.zeros(imp.shape, F32)
    for sp in range(n_sel):
        col = score[:, sp:sp + 1]
        rank = rank + ((col > score) | ((col == score) & (lane > sp))).astype(F32)
    return ((rank < min(SEL_TOPK, n_sel)) & (lane < n_sel)).astype(F32)


def _select_blocks_t(imp_t, qpos, n_sel):
    n_rows = -(-n_sel // SUBLANES) * SUBLANES
    imp_t = imp_t[0:n_rows]
    blk = _iota(imp_t.shape, 0)
    cur = lax.shift_right_arithmetic(qpos, SEL_SHIFT)
    forced = (blk == 0) | ((blk <= cur) & (blk > cur - N_LOCAL_BLOCKS))
    future = blk * SEL_BLOCK > qpos
    score = jnp.where(forced, 1e9, jnp.where(future, -1e9, imp_t))
    rank = jnp.zeros(imp_t.shape, F32)
    for sp in range(n_sel):
        row = score[sp:sp + 1, :]
        rank = rank + ((row > score) | ((row == score) & (blk > sp))).astype(F32)
    sel = ((rank < min(SEL_TOPK, n_sel)) & (blk < n_sel)).astype(F32)
    return jnp.concatenate([sel, jnp.zeros((LANES - n_rows, sel.shape[1]), F32)], axis=0)


def _gate_cols(gl, branch, reps):
    return jnp.concatenate([gl[:, r * 3 + branch:r * 3 + branch + 1] for r in range(reps)], axis=0)


def _cmp_const_kernel(pe_ref, w1_ref, c_ref):
    for kind in range(2):
        c_ref[kind] = jnp.dot(pe_ref[kind].astype(BF16), w1_ref[kind], preferred_element_type=F32)


def _cmp_const(pe_flat, w1):
    return pl.pallas_call(
        _cmp_const_kernel, grid=(1,),
        in_specs=[_const_spec(pe_flat.shape), _const_spec(w1.shape)],
        out_specs=pl.BlockSpec((2, SUBLANES, CMP_HIDDEN), lambda i: (0, 0, 0)),
        out_shape=jax.ShapeDtypeStruct((2, SUBLANES, CMP_HIDDEN), F32),
        compiler_params=_cparams("arbitrary"), name="cmp_const",
    )(pe_flat, w1)


def _compress_tail(p, q, c_ref, w2_ref, kind):
    h = p + pltpu.roll(q, q.shape[0] - 1, 0) + c_ref[kind, 0:1, :]
    return jnp.dot(jax.nn.silu(h).astype(BF16), w2_ref[kind], preferred_element_type=F32).astype(BF16)


def _compress_prompt_kernel(x_ref, w1_ref, w2_ref, c_ref, o_ref):
    nch = x_ref.shape[1] // CMP_STRIDE
    acc = [[jnp.zeros((nch, 2 * CMP_HIDDEN), F32) for _ in range(NSA_KV_HEADS)] for _ in range(2)]
    for j in range(CMP_STRIDE):
        for kind in range(2):
            xj = x_ref[kind, pl.ds(j, nch, stride=CMP_STRIDE), :].astype(BF16)
            for g in range(NSA_KV_HEADS):
                acc[kind][g] = acc[kind][g] + jnp.dot(xj[:, g * HEAD_DIM:(g + 1) * HEAD_DIM], w1_ref[kind, j],
                                                      preferred_element_type=F32)
    for kind in range(2):
        for g in range(NSA_KV_HEADS):
            a = acc[kind][g]
            o_ref[0, kind, g] = _compress_tail(a[:, :CMP_HIDDEN], a[:, CMP_HIDDEN:], c_ref, w2_ref, kind)


def _compress_prompt(cmpn, w1r, w2, cconst, n_seq):
    n = cmpn.shape[1]
    t = n // n_seq
    nch = t // CMP_STRIDE
    return pl.pallas_call(
        _compress_prompt_kernel, grid=(n_seq,),
        in_specs=[pl.BlockSpec((2, t, KV_WIDTH), lambda b: (0, b, 0)), _const_spec(w1r.shape), _const_spec(w2.shape),
                  _const_spec(cconst.shape)],
        out_specs=pl.BlockSpec((1, 2, NSA_KV_HEADS, nch, HEAD_DIM), lambda b: (b, 0, 0, 0, 0)),
        out_shape=jax.ShapeDtypeStruct((n_seq, 2, NSA_KV_HEADS, nch, HEAD_DIM), BF16),
        compiler_params=_cparams("arbitrary"), name="compress_prompt",
    )(cmpn, w1r, w2, cconst)


def _nsa_prompt_kernel(n_sel, q_ref, kc_ref, vc_ref, ks_ref, vs_ref, kw_ref, vw_ref, gates_ref, ovt_ref, ex_ref,
                       o_ref):
    qi = pl.program_id(2)
    nq, reps = QUERY_TILE, NSA_GROUP
    rows = nq * reps
    q0 = qi * nq
    qblk = q_ref[...]
    qrows = jnp.concatenate([qblk[:, r * HEAD_DIM:(r + 1) * HEAD_DIM] for r in range(reps)], axis=0)
    qpos1 = q0 + _iota((nq, 1), 0)
    qpos = jnp.concatenate([qpos1] * reps, axis=0)

    kc, vc = kc_ref[0, 0, 0], vc_ref[0, 0, 0]
    s_c = lax.dot_general(qrows, kc, _NT, preferred_element_type=F32)
    p_c = _softmax_masked(s_c, _iota((1, kc.shape[0]), 1) * CMP_STRIDE + (CMP_BLOCK - 1) <= qpos)
    o_c = jnp.dot(p_c.astype(BF16), vc, preferred_element_type=F32)
    p_sum = p_c[0:nq]
    for r in range(1, reps):
        p_sum = p_sum + p_c[r * nq:(r + 1) * nq]
    hi = p_sum.astype(BF16)
    lo = (p_sum - hi.astype(F32)).astype(BF16)
    imp_t = (lax.dot_general(ovt_ref[...], hi, _NT, preferred_element_type=F32)
             + lax.dot_general(ovt_ref[...], lo, _NT, preferred_element_type=F32))
    sel = _select_blocks_t(imp_t, q0 + _iota((1, nq), 1), n_sel).T.astype(BF16)

    def masked_scores(kt):
        s = jnp.dot(qrows, ks_ref[0, kt, 0], preferred_element_type=F32)
        picked = jnp.dot(sel, ex_ref[kt], preferred_element_type=F32) > 0.5
        picked = picked & (kt * KEY_TILE + _iota((1, KEY_TILE), 1) <= qpos1)
        return jnp.where(picked[None], s.reshape(reps, nq, KEY_TILE), NEG_BIG).reshape(rows, KEY_TILE)

    def sel_pair(i, carry):
        m, l, acc = carry
        s_a, s_b = masked_scores(2 * i), masked_scores(2 * i + 1)
        m_new = jnp.maximum(m, jnp.maximum(jnp.max(s_a, axis=-1, keepdims=True), jnp.max(s_b, axis=-1, keepdims=True)))
        alpha = jnp.exp(m - m_new)
        p_a, p_b = jnp.exp(s_a - m_new), jnp.exp(s_b - m_new)
        pv = (lax.dot_general(p_a.astype(BF16), vs_ref[0, 2 * i, 0], _NT, preferred_element_type=F32)
              + lax.dot_general(p_b.astype(BF16), vs_ref[0, 2 * i + 1, 0], _NT, preferred_element_type=F32))
        l_new = alpha * l + jnp.sum(p_a, axis=-1, keepdims=True) + jnp.sum(p_b, axis=-1, keepdims=True)
        return m_new, l_new, alpha * acc + pv

    n_tiles = q0 // KEY_TILE + 1
    _, l_s, acc_s = lax.fori_loop(0, (n_tiles + 1) // 2, sel_pair, _online_init(rows))
    o_s = acc_s / jnp.maximum(l_s, 1e-30)

    n_back = WINDOW // WIN_TILE
    tiles = [jnp.maximum(qi - n_back + d, 0) for d in range(n_back + 1)]
    s_w = jnp.concatenate([jnp.dot(qrows, kw_ref[0, jc, 0], preferred_element_type=F32) for jc in tiles], axis=1)
    wpos = (qi - n_back) * WIN_TILE + _iota((1, (n_back + 1) * WIN_TILE), 1)
    p_w = _softmax_masked(s_w, (wpos >= 0) & (wpos <= qpos) & (wpos > qpos - WINDOW)).astype(BF16)
    o_w = lax.dot_general(p_w[:, 0:WIN_TILE], vw_ref[0, tiles[0], 0], _NT, preferred_element_type=F32)
    for d in range(1, n_back + 1):
        o_w = o_w + lax.dot_general(p_w[:, d * WIN_TILE:(d + 1) * WIN_TILE], vw_ref[0, tiles[d], 0], _NT,
                                    preferred_element_type=F32)

    gl = jax.nn.sigmoid(gates_ref[...])
    o = _gate_cols(gl, 0, reps) * o_c + _gate_cols(gl, 1, reps) * o_s + _gate_cols(gl, 2, reps) * o_w
    o_ref[...] = jnp.concatenate([o[r * nq:(r + 1) * nq] for r in range(reps)], axis=1).astype(BF16)


def _nsa_prompt(q, kc, kvtb, wintb, gates, ovt, ex, n_seq):
    n = q.shape[0]
    t = n // n_seq
    nqt = t // QUERY_TILE
    nkt, nwt = kvtb.shape[1], wintb.shape[1]
    assert nkt % 2 == 0
    nch = kc.shape[3]
    g2 = NSA_KV_HEADS
    qrow = lambda w_: pl.BlockSpec((QUERY_TILE, w_), lambda b, g, i: (b * nqt + i, g))
    cmp_spec = lambda kind: pl.BlockSpec((1, 1, 1, nch, HEAD_DIM), lambda b, g, i: (b, kind, g, 0, 0))
    sel_spec = lambda kind: pl.BlockSpec((1, nkt, 1, HEAD_DIM, KEY_TILE), lambda b, g, i: (b, 0, kind * g2 + g, 0, 0))
    win_spec = lambda kind: pl.BlockSpec((1, nwt, 1, HEAD_DIM, WIN_TILE), lambda b, g, i: (b, 0, kind * g2 + g, 0, 0))
    return pl.pallas_call(
        functools.partial(_nsa_prompt_kernel, t // SEL_BLOCK),
        grid=(n_seq, g2, nqt),
        in_specs=[qrow(NSA_GROUP * HEAD_DIM), cmp_spec(0), cmp_spec(1), sel_spec(2), sel_spec(3),
                  win_spec(0), win_spec(1), qrow(LANES), _const_spec(ovt.shape), _const_spec(ex.shape)],
        out_specs=qrow(NSA_GROUP * HEAD_DIM),
        out_shape=jax.ShapeDtypeStruct((n, NSA_WIDTH), BF16),
        compiler_params=_cparams("arbitrary", "arbitrary", "arbitrary"), name="nsa_prompt",
    )(q, kc, kc, kvtb, kvtb, wintb, wintb, gates, ovt, ex)


def _nsa_sample_kernel(n_pages, n_sel, pt_ref, q_ref, kvn_ref, gates_ref, win_ref, w1_ref, w2_ref, c_ref, ov_ref,
                       ex_ref, *rest):
    del pt_ref
    pages, o_ref = rest[:n_pages], rest[n_pages]
    xt_ref, kst_ref, vst_ref = rest[n_pages + 1:]
    past = n_pages * PAGE_SIZE
    nch = past // CMP_STRIDE
    for p in range(n_pages):
        lo, hi = p * PAGE_SIZE, (p + 1) * PAGE_SIZE
        xt_ref[0, lo:hi, :] = pages[p][0, 0, 0:KV_WIDTH, :].T
        xt_ref[1, lo:hi, :] = pages[p][0, 0, KV_WIDTH:2 * KV_WIDTH, :].T
        kst_ref[:, lo:hi] = pages[p][0, 0, 2 * KV_WIDTH:3 * KV_WIDTH, :].astype(BF16)
        vst_ref[:, lo:hi] = pages[p][0, 0, 3 * KV_WIDTH:4 * KV_WIDTH, :].astype(BF16)

    cmp = [[None] * NSA_KV_HEADS for _ in range(2)]
    for kind in range(2):
        acc = jnp.zeros((nch, 2 * NSA_KV_HEADS * CMP_HIDDEN), F32)
        for jp in range(CMP_STRIDE // 2):
            a0 = xt_ref[kind, pl.ds(2 * jp, nch, stride=CMP_STRIDE), :]
            a1 = xt_ref[kind, pl.ds(2 * jp + 1, nch, stride=CMP_STRIDE), :]
            lhs = jnp.concatenate([a0, a1], axis=1).astype(BF16)
            acc = acc + jnp.dot(lhs, w1_ref[kind, jp], preferred_element_type=F32)
        for g in range(NSA_KV_HEADS):
            base = g * 2 * CMP_HIDDEN
            cmp[kind][g] = _compress_tail(acc[:, base:base + CMP_HIDDEN], acc[:, base + CMP_HIDDEN:base + 2 * CMP_HIDDEN],
                                          c_ref, w2_ref, kind)

    nt, reps = q_ref.shape[1], NSA_GROUP
    rows = nt * reps
    qb = q_ref[0]
    kvn = kvn_ref[0]
    tq1 = _iota((nt, 1), 0)
    tq = jnp.concatenate([tq1] * reps, axis=0)
    tq2 = jnp.concatenate([tq] * NSA_KV_HEADS, axis=0)
    tk = _iota((1, nt), 1)

    def qrows(g):
        return jnp.concatenate([qb[:, (g * reps + r) * HEAD_DIM:(g * reps + r + 1) * HEAD_DIM] for r in range(reps)],
                               axis=0)

    o_c, sels = [], []
    cidx = _iota((1, nch), 1)
    for g in range(NSA_KV_HEADS):
        s_c = lax.dot_general(qrows(g), cmp[0][g], _NT, preferred_element_type=F32)
        p_c = _softmax_masked(s_c, cidx * CMP_STRIDE + (CMP_BLOCK - 1) <= past + tq)
        o_c.append(jnp.dot(p_c.astype(BF16), cmp[1][g], preferred_element_type=F32))
        p_sum = p_c[0:nt]
        for r in range(1, reps):
            p_sum = p_sum + p_c[r * nt:(r + 1) * nt]
        sels.append(_select_blocks(_importance(p_sum, ov_ref[...]), past + tq1, n_sel))

    zeros = jnp.zeros((rows, HEAD_DIM), BF16)
    qbd = jnp.concatenate([jnp.concatenate([qrows(0), zeros], axis=1),
                           jnp.concatenate([zeros, qrows(1)], axis=1)], axis=0)

    def two_group_attention(s_past, mask_past, k_new, v_new, mask_new, vt_past):
        s_new = lax.dot_general(qbd, k_new, _NT, preferred_element_type=F32)
        sp = jnp.where(mask_past, s_past, NEG_BIG)
        sn = jnp.where(mask_new, s_new, NEG_BIG)
        m = jnp.maximum(jnp.max(sp, axis=-1, keepdims=True), jnp.max(sn, axis=-1, keepdims=True))
        pp = jnp.where(mask_past, jnp.exp(sp - m), 0.0)
        pn = jnp.where(mask_new, jnp.exp(sn - m), 0.0)
        den = jnp.sum(pp, axis=-1, keepdims=True) + jnp.sum(pn, axis=-1, keepdims=True)
        o2 = (lax.dot_general(pp.astype(BF16), vt_past, _NT, preferred_element_type=F32)
              + jnp.dot(pn.astype(BF16), v_new, preferred_element_type=F32)) / jnp.maximum(den, 1e-30)
        return [o2[0:rows, 0:HEAD_DIM], o2[rows:2 * rows, HEAD_DIM:2 * HEAD_DIM]]

    causal_new = tk <= tq2
    nb = past // SEL_BLOCK
    sel2 = jnp.concatenate(sels, axis=0).astype(BF16)
    picked = jnp.dot(sel2, ex_ref[...], preferred_element_type=F32) > 0.5
    mask_past = jnp.concatenate([picked[0:nt]] * reps + [picked[nt:2 * nt]] * reps, axis=0)
    new_on = jnp.concatenate([sels[g][:, nb:nb + 1] for g in range(NSA_KV_HEADS) for _ in range(reps)], axis=0) > 0.5
    o_s = two_group_attention(
        jnp.dot(qbd, kst_ref[...], preferred_element_type=F32), mask_past,
        kvn[:, 2 * KV_WIDTH:3 * KV_WIDTH].astype(BF16), kvn[:, 3 * KV_WIDTH:4 * KV_WIDTH].astype(BF16),
        new_on & causal_new, vst_ref[...])
    kwt = win_ref[0, 0, 0:KV_WIDTH, :].astype(BF16)
    vwt = win_ref[0, 0, KV_WIDTH:2 * KV_WIDTH, :].astype(BF16)
    o_w = two_group_attention(
        jnp.dot(qbd, kwt, preferred_element_type=F32), _iota((1, WINDOW), 1) > tq2,
        kvn[:, 4 * KV_WIDTH:5 * KV_WIDTH].astype(BF16), kvn[:, 5 * KV_WIDTH:6 * KV_WIDTH].astype(BF16),
        causal_new, vwt)

    gl = jax.nn.sigmoid(gates_ref[0])
    heads = []
    for g in range(NSA_KV_HEADS):
        gg = gl[:, g * LANES:(g + 1) * LANES]
        o = _gate_cols(gg, 0, reps) * o_c[g] + _gate_cols(gg, 1, reps) * o_s[g] + _gate_cols(gg, 2, reps) * o_w[g]
        heads += [o[r * nt:(r + 1) * nt] for r in range(reps)]
    o_ref[0] = jnp.concatenate(heads, axis=1).astype(BF16)


def _nsa_sample(layer, page_table, cache_t, q_b, kvn_b, gates_b, win_t, w1p, w2, cconst, ov, ex):
    bd, nt, _ = q_b.shape
    n_pages = page_table.shape[1]
    past = n_pages * PAGE_SIZE
    n_sel = -(-(past + nt) // SEL_BLOCK)
    seq = lambda w_: pl.BlockSpec((1, nt, w_), lambda s, pt: (s, 0, 0))
    page_spec = lambda p: pl.BlockSpec((1, 1, 4 * KV_WIDTH, PAGE_SIZE), lambda s, pt: (layer, pt[s, p], 0, 0))
    grid_spec = pltpu.PrefetchScalarGridSpec(
        num_scalar_prefetch=1, grid=(bd,),
        in_specs=[seq(NSA_WIDTH), seq(6 * KV_WIDTH), seq(GATE_PAD),
                  pl.BlockSpec((1, 1, 2 * KV_WIDTH, WINDOW), lambda s, pt: (layer, s, 0, 0)),
                  _const_spec(w1p.shape), _const_spec(w2.shape), _const_spec(cconst.shape), _const_spec(ov.shape),
                  _const_spec(ex.shape)] + [page_spec(p) for p in range(n_pages)],
        out_specs=seq(NSA_WIDTH),
        scratch_shapes=[pltpu.VMEM((2, past, KV_WIDTH), F32), pltpu.VMEM((KV_WIDTH, past), BF16),
                        pltpu.VMEM((KV_WIDTH, past), BF16)])
    return pl.pallas_call(
        functools.partial(_nsa_sample_kernel, n_pages, n_sel), grid_spec=grid_spec,
        out_shape=jax.ShapeDtypeStruct((bd, nt, NSA_WIDTH), BF16),
        compiler_params=_cparams("arbitrary"), name="nsa_sample",
    )(page_table, q_b, kvn_b, gates_b, win_t, w1p, w2, cconst, ov, ex, *([cache_t] * n_pages))


def _wkv_kernel(w_ref, a_ref, b_ref, k_ref, r_ref, v_ref, s0_ref, y_ref, sf_ref, s_ref):
    c = pl.program_id(1)
    tc = w_ref.shape[1]
    nj = s_ref.shape[0]
    zero = jnp.zeros(s_ref.shape[1:], F32)

    @pl.when(c == 0)
    def _():
        s_ref[...] = s0_ref[0]

    sa0 = zero
    for j in range(nj):
        sa0 = sa0 + s_ref[j] * a_ref[0, 0, j:j + 1, :]

    def step(t, sa):
        tn = jnp.minimum(t + 1, tc - 1)
        v_t = v_ref[0, t]
        y, sa_next = zero, zero
        for j in range(nj):
            s = s_ref[j] * w_ref[0, t, j:j + 1, :] + sa * b_ref[0, t, j:j + 1, :] + v_t * k_ref[0, t, j:j + 1, :]
            s_ref[j] = s
            y = y + s * r_ref[0, t, j:j + 1, :]
            sa_next = sa_next + s * a_ref[0, tn, j:j + 1, :]
        y_ref[0, t] = y
        return sa_next

    lax.fori_loop(0, tc, step, sa0)

    @pl.when(c == pl.num_programs(1) - 1)
    def _():
        sf_ref[0] = s_ref[...]


def _wkv(w, a, b, k, r, v, s0, tc):
    g, t, nj, lanes = w.shape
    ni = v.shape[2]
    byj = pl.BlockSpec((1, tc, nj, lanes), lambda gi, c: (gi, c, 0, 0))
    byi = pl.BlockSpec((1, tc, ni, lanes), lambda gi, c: (gi, c, 0, 0))
    st = pl.BlockSpec((1, nj, ni, lanes), lambda gi, c: (gi, 0, 0, 0))
    return pl.pallas_call(
        _wkv_kernel, grid=(g, t // tc),
        in_specs=[byj] * 5 + [byi, st], out_specs=[byi, st],
        out_shape=[jax.ShapeDtypeStruct(v.shape, F32), jax.ShapeDtypeStruct(s0.shape, F32)],
        scratch_shapes=[pltpu.VMEM((nj, ni, lanes), F32)],
        compiler_params=_cparams("arbitrary", "arbitrary"), name="wkv_scan",
    )(w, a, b, k, r, v, s0)


def _head_sum(z, bd):
    hi = z.astype(BF16)
    lo = (z - hi.astype(F32)).astype(BF16)
    return jnp.dot(hi, bd, preferred_element_type=F32) + jnp.dot(lo, bd, preferred_element_type=F32)


def _rwkv_pre_kernel(tps, x_ref, *refs):
    if tps:
        (mu_ref, w0_ref, wb_ref, a0_ref, ab_ref, gb_ref, kk_ref, ka_ref, rk_ref, bd_ref,
         r_ref, w_ref, k_ref, v_ref, a_ref, b_ref, g_ref, bonus_ref, shift_ref, scr) = refs
        tm = x_ref.shape[0]

        @pl.when(pl.program_id(0) % tps == 0)
        def _():
            scr[0:SUBLANES, :] = jnp.zeros((SUBLANES, scr.shape[1]), F32)

        x = x_ref[...]
        scr[SUBLANES:SUBLANES + tm, :] = x
        prev = scr[SUBLANES - 1:SUBLANES - 1 + tm, :]
        last = scr[SUBLANES - 1 + tm:SUBLANES + tm, :]
        shift_ref[0] = last
        scr[SUBLANES - 1:SUBLANES, :] = last
    else:
        (st_ref, mu_ref, w0_ref, wb_ref, a0_ref, ab_ref, gb_ref, kk_ref, ka_ref, rk_ref, bd_ref,
         r_ref, w_ref, k_ref, v_ref, a_ref, b_ref, g_ref, bonus_ref, shift_ref, scr) = refs

        @pl.when(pl.program_id(0) == 0)
        def _():
            scr[...] = st_ref[...]

        x = x_ref[...]
        prev = scr[...]
        scr[...] = x
        shift_ref[...] = x
    xm = x + (prev - x) * mu_ref[...]
    hw = RWKV_WIDTH
    r, k, v = xm[:, 0:hw], xm[:, hw:2 * hw], xm[:, 2 * hw:3 * hw]
    lora = xm[:, RW_LORA:RW_LORA + LANES]
    ga = xm[:, RW_LORA + LANES:RW_LORA + LANES + GATE_LORA]
    bd = bd_ref[...]
    w = -jax.nn.softplus(-(w0_ref[...] + jnp.dot(jnp.tanh(lora).astype(BF16), wb_ref[...],
                                                   preferred_element_type=F32))) - 0.5
    a = jax.nn.sigmoid(a0_ref[...] + jnp.dot(lora.astype(BF16), ab_ref[...], preferred_element_type=F32))
    kk = k * kk_ref[...]
    kk = kk / jnp.maximum(jnp.sqrt(_head_sum(kk * kk, bd)), 1e-12)
    k_h = k * (1.0 + (a - 1.0) * ka_ref[...])
    r_ref[...] = r
    w_ref[...] = jnp.exp(-jnp.exp(w))
    k_ref[...] = k_h
    v_ref[...] = v
    a_ref[...] = -kk
    b_ref[...] = kk * a
    g_ref[...] = jnp.dot(jax.nn.sigmoid(ga).astype(BF16), gb_ref[...], preferred_element_type=F32)
    bonus_ref[...] = _head_sum(r * k_h * rk_ref[...], bd) * v


def _rwkv_pre(prw, lw, *, n_seq, tm, shift_state=None):
    n = prw.shape[0]
    nt = n // tm
    row = lambda w_: pl.BlockSpec((tm, w_), lambda i: (i, 0))
    consts = [lw[k] for k in ('mu_p', 'w0', 'wb_p', 'a0', 'ab_p', 'gb', 'k_k', 'k_a', 'r_k', 'bd')]
    in_specs = [row(RW_COLS)] + ([] if shift_state is None else [_const_spec(shift_state.shape)]) \
        + [_const_spec(c.shape) for c in consts]
    args = [prw] + ([] if shift_state is None else [shift_state]) + consts
    feat = jax.ShapeDtypeStruct((n, RWKV_WIDTH), F32)
    if shift_state is None:
        tps = nt // n_seq
        shift_shape = jax.ShapeDtypeStruct((n_seq, 1, RW_COLS), F32)
        shift_spec = pl.BlockSpec((1, 1, RW_COLS), lambda i: (i // tps, 0, 0))
        scratch = [pltpu.VMEM((tm + SUBLANES, RW_COLS), F32)]
    else:
        tps = 0
        shift_shape = jax.ShapeDtypeStruct((tm, RW_COLS), F32)
        shift_spec = pl.BlockSpec((tm, RW_COLS), lambda i: (0, 0))
        scratch = [pltpu.VMEM((tm, RW_COLS), F32)]
    return pl.pallas_call(
        functools.partial(_rwkv_pre_kernel, tps), grid=(nt,), in_specs=in_specs,
        out_specs=[row(RWKV_WIDTH)] * 8 + [shift_spec], out_shape=[feat] * 8 + [shift_shape],
        scratch_shapes=scratch, compiler_params=_cparams("arbitrary"),
        name="rwkv_pre_prompt" if tps else "rwkv_pre_sample",
    )(*args)


def _rwkv_post_kernel(y_ref, bonus_ref, g_ref, gw_ref, gb_ref, bd_ref, o_ref):
    y = y_ref[...]
    bd = bd_ref[...]
    d = y - _head_sum(y, bd) * (1.0 / HEAD_DIM)
    var = _head_sum(d * d, bd) * (1.0 / HEAD_DIM)
    yn = d * lax.rsqrt(var + GN_EPS) * gw_ref[...] + gb_ref[...]
    o_ref[...] = ((yn + bonus_ref[...]) * g_ref[...]).astype(BF16)


def _rwkv_post(y, bonus, g, lw, tm):
    n = y.shape[0]
    row = pl.BlockSpec((tm, RWKV_WIDTH), lambda i: (i, 0))
    consts = [lw['gn_w'], lw['gn_b'], lw['bd']]
    return pl.pallas_call(
        _rwkv_post_kernel, grid=(n // tm,), in_specs=[row] * 3 + [_const_spec(c.shape) for c in consts],
        out_specs=row, out_shape=jax.ShapeDtypeStruct((n, RWKV_WIDTH), BF16),
        compiler_params=_cparams("arbitrary"), name="rwkv_post",
    )(y, bonus, g, *consts)


def _rope_tables(pos):
    half = HEAD_DIM // 2
    inv = ROPE_THETA ** (-jnp.arange(half, dtype=F32) / half)
    ang = pos.astype(F32)[:, None] * inv[None, :]
    cos, sin = jnp.cos(ang), jnp.sin(ang)
    reps = LANES // HEAD_DIM
    return jnp.tile(jnp.concatenate([cos, cos], axis=1), (1, reps)), jnp.tile(jnp.concatenate([-sin, sin], axis=1), (1, reps))


def _overlap(n_rows, n_cmp, n_sel):
    c = jnp.arange(n_rows)[:, None]
    s = jnp.arange(LANES)[None, :]
    hit = (c * CMP_STRIDE < (s + 1) * SEL_BLOCK) & (c * CMP_STRIDE + CMP_BLOCK - 1 >= s * SEL_BLOCK)
    return (hit & (c < n_cmp) & (s < n_sel)).astype(BF16)


def _prep_layer(l, p):
    w_in = p['w_in'][l]
    q_cols = w_in[:, :NSA_WIDTH]
    kv_cols = w_in[:, NSA_WIDTH:NSA_WIDTH + 6 * KV_WIDTH]
    g_cols = w_in[:, NSA_WIDTH + 6 * KV_WIDTH:NSA_WIDTH + 6 * KV_WIDTH + N_GATES]
    rw_cols = w_in[:, NSA_WIDTH + 6 * KV_WIDTH + N_GATES:]
    per_group = N_GATES // NSA_KV_HEADS
    g_pad = [jnp.pad(g_cols[:, g * per_group:(g + 1) * per_group], ((0, 0), (0, LANES - per_group)))
             for g in range(NSA_KV_HEADS)]
    fpad = D_FF_PAD - D_FF
    w1 = p['cmp_w1'][l].astype(BF16)
    w1s = w1.reshape(2, CMP_BLOCK, HEAD_DIM, CMP_HIDDEN)
    w1r = jnp.concatenate([w1s[:, :CMP_STRIDE], w1s[:, CMP_STRIDE:]], axis=-1)
    z = jnp.zeros_like(w1r[:, 0::2])
    w1p = jnp.concatenate([
        jnp.concatenate([w1r[:, 0::2], z], axis=-1), jnp.concatenate([z, w1r[:, 0::2]], axis=-1),
        jnp.concatenate([w1r[:, 1::2], z], axis=-1), jnp.concatenate([z, w1r[:, 1::2]], axis=-1)], axis=2)
    pe_flat = jnp.broadcast_to(p['cmp_pe'][l].reshape(2, 1, CMP_BLOCK * HEAD_DIM), (2, SUBLANES, CMP_BLOCK * HEAD_DIM))
    zl = jnp.zeros((DECAY_LORA, RWKV_WIDTH), F32)
    head = jnp.arange(RWKV_WIDTH) // HEAD_DIM
    lw = dict(
        mu_p=_rw_permute(p['rw_mu'][l])[None], w0=p['rw_w0'][l][None], a0=p['rw_a0'][l][None],
        wb_p=jnp.concatenate([p['rw_w_b'][l], zl], axis=0).astype(BF16),
        ab_p=jnp.concatenate([zl, p['rw_a_b'][l]], axis=0).astype(BF16),
        gb=p['rw_g_b'][l].astype(BF16), k_k=p['rw_k_k'][l][None], k_a=p['rw_k_a'][l][None],
        r_k=p['rw_r_k'][l].reshape(1, RWKV_WIDTH), gn_w=p['rw_gn_w'][l][None], gn_b=p['rw_gn_b'][l][None],
        bd=(head[:, None] == head[None, :]).astype(BF16))
    lw.update(
        norm_mix=p['norm_mix'][l][None], norm_ffn=p['norm_ffn'][l][None],
        w_in=jnp.concatenate([q_cols, kv_cols, _rw_permute(rw_cols)] + g_pad, axis=1).astype(BF16),
        w1r=w1r, w1p=w1p, w2=p['cmp_w2'][l].astype(BF16), cconst=_cmp_const(pe_flat, w1),
        w_out=p['w_out'][l].astype(BF16),
        wg=jnp.pad(p['ffn_w_gate'][l], ((0, 0), (0, fpad))).astype(BF16),
        wu=jnp.pad(p['ffn_w_up'][l], ((0, 0), (0, fpad))).astype(BF16),
        wd=jnp.pad(p['ffn_w_down'][l], ((0, fpad), (0, 0))).astype(BF16),
        cw=jnp.pad(p['ffn_conv_w'][l], ((0, 0), (0, fpad))), cb=jnp.pad(p['ffn_conv_b'][l], (0, fpad))[None])
    return lw


def _layer_prompt(x, lw, cos, sin, ovt, ex, b, t):
    n = b * t
    q, kvt, kvtb, wint, wintb, cmpn, gates, prw = _in_proj(
        x, lw['norm_mix'], lw['w_in'], cos, sin, prompt=True, n_seq=b, tm=TM_PROMPT)
    kc = _compress_prompt(cmpn, lw['w1r'], lw['w2'], lw['cconst'], b)
    o_nsa = _nsa_prompt(q, kc, kvtb, wintb, gates, ovt, ex, b)

    r_, w_, k_, v_, a_, b_, g, bonus, shift = _rwkv_pre(prw, lw, n_seq=b, tm=TM_PROMPT)
    half = HEAD_DIM // 2
    lanes = 2 * b * RWKV_HEADS

    def by_key(z):
        zt = z.reshape(b, t, RWKV_HEADS, HEAD_DIM).transpose(1, 3, 0, 2).reshape(t, HEAD_DIM, b * RWKV_HEADS)
        return jnp.concatenate([zt, zt], axis=-1)[None]

    v_p = v_.reshape(b, t, RWKV_HEADS, 2, half).transpose(1, 4, 3, 0, 2).reshape(1, t, half, lanes)
    y_p, s_fin = _wkv(by_key(w_), by_key(a_), by_key(b_), by_key(k_), by_key(r_), v_p,
                      jnp.zeros((1, HEAD_DIM, half, lanes), F32), tc=min(t, 32))
    y = y_p.reshape(t, half, 2, b, RWKV_HEADS).transpose(3, 0, 4, 2, 1).reshape(n, RWKV_WIDTH)
    wkv_new = s_fin.reshape(HEAD_DIM, half, 2, b, RWKV_HEADS).transpose(3, 4, 2, 1, 0).reshape(
        b, RWKV_HEADS, HEAD_DIM, HEAD_DIM)
    o_rw = _rwkv_post(y, bonus, g, lw, TM_PROMPT)

    x_new, conv_new = _ffn(x, o_nsa, o_rw, lw['w_out'], lw['norm_ffn'], lw['wg'], lw['wu'], lw['cw'], lw['cb'],
                           lw['wd'], n_seq=b, tm=TM_PROMPT)
    rows = kvt.reshape(b, 4, NSA_KV_HEADS, HEAD_DIM, t).transpose(0, 4, 1, 2, 3)
    win = wint.reshape(b, 2, NSA_KV_HEADS, HEAD_DIM, WINDOW).transpose(0, 4, 1, 2, 3)
    return x_new, rows, win, wkv_new, _rw_unpermute(shift[:, 0]), conv_new[:, :, :D_FF]


def _layer_sample(x, lw, l, cos, sin, ov, ex, page_table, cache_t, win_t, state_win, state_wkv, state_shift,
                  state_conv, bd, t):
    n = bd * t
    q, kvn, gates, prw = _in_proj(x, lw['norm_mix'], lw['w_in'], cos, sin, prompt=False, n_seq=bd, tm=bd)
    to_b = lambda z: z.reshape(t, bd, z.shape[-1]).transpose(1, 0, 2)
    kvn_b = to_b(kvn)
    o_b = _nsa_sample(l, page_table, cache_t, to_b(q), kvn_b, to_b(gates), win_t, lw['w1p'], lw['w2'], lw['cconst'],
                      ov, ex)
    o_nsa = o_b.transpose(1, 0, 2).reshape(n, NSA_WIDTH)

    r_, w_, k_, v_, a_, b_, g, bonus, shift = _rwkv_pre(
        prw, lw, n_seq=bd, tm=bd, shift_state=_rw_permute(state_shift[l]))
    by_key = lambda z: z.reshape(t, bd, RWKV_HEADS, HEAD_DIM).transpose(2, 0, 3, 1)
    y_p, s_fin = _wkv(by_key(w_), by_key(a_), by_key(b_), by_key(k_), by_key(r_), by_key(v_),
                      state_wkv[l].transpose(1, 3, 2, 0), tc=t)
    y = y_p.transpose(1, 3, 0, 2).reshape(n, RWKV_WIDTH)
    wkv_new = s_fin.transpose(3, 0, 2, 1)
    o_rw = _rwkv_post(y, bonus, g, lw, bd)

    conv_state = jnp.pad(state_conv[l].transpose(1, 0, 2), ((0, 0), (0, 0), (0, D_FF_PAD - D_FF)))
    x_new, conv_new = _ffn(x, o_nsa, o_rw, lw['w_out'], lw['norm_ffn'], lw['wg'], lw['wu'], lw['cw'], lw['cb'],
                           lw['wd'], n_seq=bd, tm=bd, conv_state=conv_state)
    rows = kvn_b[:, :, :4 * KV_WIDTH].reshape(bd, t, 4, NSA_KV_HEADS, HEAD_DIM)
    win_rows = kvn_b[:, :, 4 * KV_WIDTH:].reshape(bd, t, 2, NSA_KV_HEADS, HEAD_DIM)
    win_valid = jnp.concatenate([state_win[l], win_rows], axis=1)
    win = win_valid[:, win_valid.shape[1] - min(WINDOW, win_valid.shape[1]):]
    return x_new, rows, win, wkv_new, _rw_unpermute(shift), conv_new.transpose(1, 0, 2)[:, :, :D_FF]


def kernel(x_prompt, x_sample, cache_kv, page_table, state_win, state_wkv, state_shift, state_conv,
           norm_mix, w_in, cmp_pe, cmp_w1, cmp_w2, rw_mu, rw_w0, rw_w_b, rw_a0, rw_a_b, rw_g_b,
           rw_k_k, rw_k_a, rw_r_k, rw_gn_w, rw_gn_b, w_out, norm_ffn, ffn_w_gate, ffn_w_up,
           ffn_conv_w, ffn_conv_b, ffn_w_down, norm_final):
    p = dict(norm_mix=norm_mix, w_in=w_in, cmp_pe=cmp_pe, cmp_w1=cmp_w1, cmp_w2=cmp_w2, rw_mu=rw_mu, rw_w0=rw_w0,
             rw_w_b=rw_w_b, rw_a0=rw_a0, rw_a_b=rw_a_b, rw_g_b=rw_g_b, rw_k_k=rw_k_k, rw_k_a=rw_k_a, rw_r_k=rw_r_k,
             rw_gn_w=rw_gn_w, rw_gn_b=rw_gn_b, w_out=w_out, norm_ffn=norm_ffn, ffn_w_gate=ffn_w_gate,
             ffn_w_up=ffn_w_up, ffn_conv_w=ffn_conv_w, ffn_conv_b=ffn_conv_b, ffn_w_down=ffn_w_down)
    depth = w_in.shape[0]
    b, t, d = x_prompt.shape
    bd, ts, _ = x_sample.shape
    n_pages = page_table.shape[1]
    past = n_pages * PAGE_SIZE
    n_phys = cache_kv.shape[1]
    assert d == D_MODEL and t % TM_PROMPT == 0 and t >= WINDOW and TM_PROMPT == WINDOW
    assert state_win.shape[2] == WINDOW and past >= WINDOW and past % SEL_BLOCK == 0 and ts <= SUBLANES
    assert bd % SUBLANES == 0 and cache_kv.shape[2] == PAGE_SIZE
    n_cmp_s = (past + ts - CMP_BLOCK) // CMP_STRIDE + 1
    assert (n_cmp_s - 1) * CMP_STRIDE + CMP_BLOCK <= past

    cos_p, sin_p = (z.reshape(t // TM_PROMPT, TM_PROMPT, LANES) for z in _rope_tables(jnp.arange(t)))
    cos_s, sin_s = (z.reshape(ts, 1, LANES) for z in _rope_tables(past + jnp.arange(ts)))
    ovt_p = _overlap(t // CMP_STRIDE, (t - CMP_BLOCK) // CMP_STRIDE + 1, t // SEL_BLOCK).T
    ov_s = _overlap(past // CMP_STRIDE, n_cmp_s, -(-(past + ts) // SEL_BLOCK))
    expand = lambda n_keys: (jnp.arange(LANES)[:, None] == (jnp.arange(n_keys) // SEL_BLOCK)[None, :]).astype(BF16)
    ex_p = expand(t).reshape(LANES, t // KEY_TILE, KEY_TILE).transpose(1, 0, 2)
    ex = expand(past)
    cache_t = cache_kv.transpose(0, 1, 3, 4, 5, 2).reshape(depth, n_phys, 4 * KV_WIDTH, PAGE_SIZE)
    win_t = state_win.transpose(0, 1, 3, 4, 5, 2).reshape(depth, bd, 2 * KV_WIDTH, WINDOW)

    x_p = x_prompt.reshape(b * t, d)
    x_s = x_sample.transpose(1, 0, 2).reshape(ts * bd, d)
    outs = [[] for _ in range(10)]
    for l in range(depth):
        lw = _prep_layer(l, p)
        x_p, *res_p = _layer_prompt(x_p, lw, cos_p, sin_p, ovt_p, ex_p, b, t)
        x_s, *res_s = _layer_sample(x_s, lw, l, cos_s, sin_s, ov_s, ex, page_table, cache_t, win_t, state_win,
                                    state_wkv, state_shift, state_conv, bd, ts)
        for i in range(5):
            outs[2 * i].append(res_p[i])
            outs[2 * i + 1].append(res_s[i])
    g_fin = norm_final[None]
    y_prompt = _final_norm(x_p, g_fin, TM_PROMPT).reshape(b, t, d)
    y_sample = _final_norm(x_s, g_fin, bd).reshape(ts, bd, d).transpose(1, 0, 2)
    return (y_prompt, y_sample) + tuple(jnp.stack(o) for o in outs)
```
